```python
import numpy as np
import jax
import jax.numpy as jnp
from jax import lax


D_MODEL = 1024
BATCH = 8
SEQ = 2048
DEPTH = 2

HEAD_DIM = 64
A_HEADS = D_MODEL // (4 * HEAD_DIM)
IDX_HEADS = 4
IDX_DIM = 64
DSA_TOPK = 256
B_HEADS = D_MODEL // (2 * HEAD_DIM)
B_WIDTH = B_HEADS * HEAD_DIM
DECAY_LORA = 64
AAA_LORA = 64
GATE_LORA = 128
GN_EPS = 64e-5
C_HEADS = D_MODEL // (4 * HEAD_DIM)
CMP_LEN = 32
CMP_STRIDE = 16
CMP_HID = 256
SEL_LEN = 64
SEL_TOPN = 16
WIN = 512
Q_BLOCK = 128
ROPE_THETA = 10000.0
NORM_EPS = 1e-6
NEG = -1e30
MIX_WIDTH = A_HEADS * HEAD_DIM + B_WIDTH + C_HEADS * HEAD_DIM
A_COLS = (A_HEADS * HEAD_DIM, HEAD_DIM, HEAD_DIM, IDX_HEADS * IDX_DIM, IDX_DIM, IDX_HEADS)
B_COLS = (B_WIDTH, B_WIDTH, B_WIDTH, DECAY_LORA, AAA_LORA, GATE_LORA)
C_COLS = (C_HEADS * HEAD_DIM, 6 * HEAD_DIM, 3 * C_HEADS)
IN_COLS = sum(A_COLS) + sum(B_COLS) + sum(C_COLS)
FFN_HIDDEN = ((8 * D_MODEL // 3 + 255) // 256) * 256

kernel_name = 'hymba_dsa_rwkv7_nsa_block'


def rms_norm(x, g, eps=NORM_EPS):
    xf = x.astype(jnp.float32)
    y = xf * lax.rsqrt(jnp.mean(xf * xf, axis=-1, keepdims=True) + eps)
    return (y * g.astype(jnp.float32)).astype(x.dtype)


def rope(x, pos):
    half = x.shape[-1] // 2
    inv = ROPE_THETA ** (-jnp.arange(half, dtype=jnp.float32) / half)
    ang = pos.astype(jnp.float32)[..., None] * inv
    cos, sin = jnp.cos(ang), jnp.sin(ang)
    xf = x.astype(jnp.float32)
    x1, x2 = xf[..., :half], xf[..., half:]
    return jnp.concatenate([x1 * cos - x2 * sin, x2 * cos + x1 * sin], axis=-1).astype(x.dtype)


def split_cols(z, sizes):
    return jnp.split(z, np.cumsum(sizes)[:-1].tolist(), axis=-1)


def attend(q, k, v, valid):
    s = jnp.einsum('bqhd,bqkd->bqhk', q, k, preferred_element_type=jnp.float32) * HEAD_DIM ** -0.5
    s = jnp.where(valid[:, :, None, :], s, -jnp.inf)
    p = jax.nn.softmax(s, axis=-1).astype(v.dtype)
    return jnp.einsum('bqhk,bqkd->bqhd', p, v)


def dsa_mixer(q, k, v, iq, ik, iw, q_g, k_g):
    B, T = q.shape[:2]
    pos = jnp.arange(T)
    q = rope(rms_norm(q, q_g), pos[:, None])
    k = rope(rms_norm(k, k_g), pos)
    iq = rope(iq, pos[:, None]).astype(jnp.float32)
    ik = rope(ik, pos).astype(jnp.float32)
    iw = iw.astype(jnp.float32)
    topk = min(DSA_TOPK, T // 4)

    def block(i):
        t0 = i * Q_BLOCK
        tq = t0 + jnp.arange(Q_BLOCK)
        qb = lax.dynamic_slice_in_dim(q, t0, Q_BLOCK, axis=1)
        iqb = lax.dynamic_slice_in_dim(iq, t0, Q_BLOCK, axis=1)
        iwb = lax.dynamic_slice_in_dim(iw, t0, Q_BLOCK, axis=1)
        score = jnp.einsum('bqh,bqhs->bqs', iwb,
                           jax.nn.relu(jnp.einsum('bqhd,bsd->bqhs', iqb, ik)))
        score = jnp.where(pos[None, None, :] <= tq[None, :, None], score, -jnp.inf)
        _, idx = lax.top_k(score, topk)
        ks = jax.vmap(lambda kb, ib: kb[ib])(k, idx)
        vs = jax.vmap(lambda vb, ib: vb[ib])(v, idx)
        return attend(qb, ks, vs, idx <= tq[None, :, None])

    o = lax.map(block, jnp.arange(T // Q_BLOCK))
    return jnp.moveaxis(o, 0, 1).reshape(B, T, A_HEADS * HEAD_DIM)


def rwkv7_mixer(zb, mu, w0, w2, a0, a2, g2, k_k, k_a, r_k, ln_w, ln_b):
    B, T, _ = zb.shape
    f32 = jnp.float32
    z_prev = jnp.pad(zb, ((0, 0), (1, 0), (0, 0)))[:, :-1]
    z = zb + (z_prev - zb) * mu
    r, k, v, wd, ad, gd = split_cols(z, B_COLS)
    w_log = -jax.nn.softplus(-(w0 + jnp.tanh(wd) @ w2).astype(f32)) - 0.5
    decay = jnp.exp(-jnp.exp(w_log))
    a = jax.nn.sigmoid((a0 + ad @ a2).astype(f32))
    g = jax.nn.sigmoid(gd) @ g2
    hs = lambda t: t.reshape(B, T, B_HEADS, HEAD_DIM)
    kk = hs((k * k_k).astype(f32))
    kk = kk / jnp.maximum(jnp.linalg.norm(kk, axis=-1, keepdims=True), 1e-12)
    k = hs(k.astype(f32) * (1.0 + (a - 1.0) * k_a))
    r, v, a, decay = hs(r.astype(f32)), hs(v.astype(f32)), hs(a), hs(decay)

    def step(S, inp):
        r_t, w_t, k_t, v_t, kk_t, a_t = inp
        sa = jnp.einsum('bhij,bhj->bhi', S, kk_t)
        S = (S * w_t[:, :, None, :] - sa[..., None] * (kk_t * a_t)[:, :, None, :]
             + v_t[..., None] * k_t[:, :, None, :])
        return S, jnp.einsum('bhij,bhj->bhi', S, r_t)

    xs = tuple(jnp.moveaxis(t, 1, 0) for t in (r, decay, k, v, kk, a))
    S0 = jnp.zeros((B, B_HEADS, HEAD_DIM, HEAD_DIM), f32)
    _, y = lax.scan(step, S0, xs)
    y = jnp.moveaxis(y, 0, 1)
    mean = jnp.mean(y, axis=-1, keepdims=True)
    var = jnp.mean((y - mean) ** 2, axis=-1, keepdims=True)
    y = ((y - mean) * lax.rsqrt(var + GN_EPS)).reshape(B, T, B_WIDTH) * ln_w + ln_b
    bonus = jnp.sum(r * k * r_k, axis=-1, keepdims=True) * v
    y = (y + bonus.reshape(B, T, B_WIDTH)) * g
    return y.astype(zb.dtype)


def nsa_mixer(q, kc, vc, ks, vs, kw, vw, gates, q_g, k_g, pe, w1, w2):
    B, T = q.shape[:2]
    f32 = jnp.float32
    pos = jnp.arange(T)
    scale = HEAD_DIM ** -0.5
    q = rope(rms_norm(q, q_g), pos[:, None])
    n_cmp = (T - CMP_LEN) // CMP_STRIDE + 1
    starts = np.arange(n_cmp) * CMP_STRIDE
    end_pos = starts + CMP_LEN - 1
    blk_idx = starts[:, None] + np.arange(CMP_LEN)[None, :]

    def compress(x, j):
        blk = x[:, blk_idx] + pe[j]
        h = jax.nn.gelu(blk.reshape(B, n_cmp, CMP_LEN * HEAD_DIM) @ w1[j])
        return h @ w2[j]

    k_cmp = rope(rms_norm(compress(kc, 0), k_g), jnp.asarray(end_pos))
    v_cmp = compress(vc, 1)
    s = jnp.einsum('bthd,bnd->bhtn', q, k_cmp, preferred_element_type=f32) * scale
    cmp_valid = jnp.asarray(end_pos)[None, :] <= pos[:, None]
    p = jax.nn.softmax(jnp.where(cmp_valid, s, NEG), axis=-1) * cmp_valid
    o_cmp = jnp.einsum('bhtn,bnd->bthd', p.astype(vc.dtype), v_cmp)
    n_blk = T // SEL_LEN
    sel_start = np.arange(n_blk) * SEL_LEN
    overlap = ((starts[:, None] <= sel_start[None, :] + SEL_LEN - 1)
               & (end_pos[:, None] >= sel_start[None, :])).astype(np.float32)
    imp = jnp.einsum('bhtn,nj->btj', p, jnp.asarray(overlap))
    blk = jnp.arange(n_blk)
    cur = pos // SEL_LEN
    admissible = blk[None, :] * SEL_LEN <= pos[:, None]
    forced = (blk[None, :] == 0) | (blk[None, :] == cur[:, None]) | (blk[None, :] == cur[:, None] - 1)
    imp = jnp.where(admissible, jnp.where(forced, jnp.inf, imp), -jnp.inf)
    top_n = min(SEL_TOPN, n_blk)
    _, sel = lax.top_k(imp, top_n)
    ks = rope(rms_norm(ks, k_g), pos)
    kw = rope(rms_norm(kw, k_g), pos)
    ks_blk = ks.reshape(B, n_blk, SEL_LEN, HEAD_DIM)
    vs_blk = vs.reshape(B, n_blk, SEL_LEN, HEAD_DIM)
    kw_pad = jnp.pad(kw, ((0, 0), (WIN, 0), (0, 0)))
    vw_pad = jnp.pad(vw, ((0, 0), (WIN, 0), (0, 0)))
    in_blk = jnp.arange(SEL_LEN)

    def block(i):
        t0 = i * Q_BLOCK
        tq = t0 + jnp.arange(Q_BLOCK)
        qb = lax.dynamic_slice_in_dim(q, t0, Q_BLOCK, axis=1)
        selb = lax.dynamic_slice_in_dim(sel, t0, Q_BLOCK, axis=1)
        kg = jax.vmap(lambda kb, ib: kb[ib])(ks_blk, selb).reshape(B, Q_BLOCK, top_n * SEL_LEN, HEAD_DIM)
        vg = jax.vmap(lambda vb, ib: vb[ib])(vs_blk, selb).reshape(B, Q_BLOCK, top_n * SEL_LEN, HEAD_DIM)
        kpos = (selb[..., None] * SEL_LEN + in_blk).reshape(B, Q_BLOCK, top_n * SEL_LEN)
        o_slc = attend(qb, kg, vg, kpos <= tq[None, :, None])
        kwb = lax.dynamic_slice_in_dim(kw_pad, t0, WIN + Q_BLOCK, axis=1)
        vwb = lax.dynamic_slice_in_dim(vw_pad, t0, WIN + Q_BLOCK, axis=1)
        wpos = t0 - WIN + jnp.arange(WIN + Q_BLOCK)
        wvalid = ((wpos[None, :] <= tq[:, None]) & (wpos[None, :] > tq[:, None] - WIN)
                  & (wpos[None, :] >= 0))
        sw = jnp.einsum('bqhd,bkd->bqhk', qb, kwb, preferred_element_type=f32) * scale
        sw = jnp.where(wvalid[None, :, None, :], sw, -jnp.inf)
        o_win = jnp.einsum('bqhk,bkd->bqhd', jax.nn.softmax(sw, axis=-1).astype(vwb.dtype), vwb)
        return o_slc, o_win

    o_slc, o_win = lax.map(block, jnp.arange(T // Q_BLOCK))
    unblock = lambda o: jnp.moveaxis(o, 0, 1).reshape(B, T, C_HEADS, HEAD_DIM)
    g = jax.nn.sigmoid(gates.astype(f32)).reshape(B, T, C_HEADS, 3)
    o = g[..., 0:1] * o_cmp + g[..., 1:2] * unblock(o_slc) + g[..., 2:3] * unblock(o_win)
    return o.reshape(B, T, C_HEADS * HEAD_DIM).astype(q.dtype)


def setup_inputs(seed: int = 0) -> dict:
    key = jax.random.key(seed)
    ks = iter(jax.random.split(key, 32))
    nrm = lambda shape, s: jax.random.normal(next(ks), shape, jnp.float32) * s
    L, D = DEPTH, D_MODEL
    return {
        'x': nrm((BATCH, SEQ, D), 1.0),
        'c': nrm((BATCH, D), 1.0),
        'ada_w': nrm((L, D, 6 * D), 0.5 * D ** -0.5),
        'ada_b': nrm((L, 6 * D), 0.01),
        'norm1_g': 1.0 + nrm((L, D), 0.02),
        'w_in': nrm((L, D, IN_COLS), D ** -0.5),
        'dsa_q_g': 1.0 + nrm((L, HEAD_DIM), 0.02),
        'dsa_k_g': 1.0 + nrm((L, HEAD_DIM), 0.02),
        'rwkv_mu': jax.random.uniform(next(ks), (L, sum(B_COLS)), jnp.float32),
        'rwkv_w0': nrm((L, B_WIDTH), 0.5),
        'rwkv_w2': nrm((L, DECAY_LORA, B_WIDTH), DECAY_LORA ** -0.5),
        'rwkv_a0': nrm((L, B_WIDTH), 0.1),
        'rwkv_a2': nrm((L, AAA_LORA, B_WIDTH), AAA_LORA ** -0.5),
        'rwkv_g2': nrm((L, GATE_LORA, B_WIDTH), GATE_LORA ** -0.5),
        'rwkv_k_k': 0.85 + nrm((L, B_WIDTH), 0.05),
        'rwkv_k_a': 1.0 + nrm((L, B_WIDTH), 0.05),
        'rwkv_r_k': nrm((L, B_HEADS, HEAD_DIM), 0.1),
        'rwkv_ln_w': 1.0 + nrm((L, B_WIDTH), 0.02),
        'rwkv_ln_b': nrm((L, B_WIDTH), 0.01),
        'nsa_q_g': 1.0 + nrm((L, HEAD_DIM), 0.02),
        'nsa_k_g': 1.0 + nrm((L, HEAD_DIM), 0.02),
        'nsa_pe': nrm((L, 2, CMP_LEN, HEAD_DIM), 0.02),
        'nsa_w1': nrm((L, 2, CMP_LEN * HEAD_DIM, CMP_HID), (CMP_LEN * HEAD_DIM) ** -0.5),
        'nsa_w2': nrm((L, 2, CMP_HID, HEAD_DIM), CMP_HID ** -0.5),
        'w_out': nrm((L, MIX_WIDTH, D), MIX_WIDTH ** -0.5),
        'norm2_g': 1.0 + nrm((L, D), 0.02),
        'ffn_wi': nrm((L, D, 2 * FFN_HIDDEN), D ** -0.5),
        'ffn_wo': nrm((L, FFN_HIDDEN, D), FFN_HIDDEN ** -0.5),
    }


def reference(x, c, ada_w, ada_b, norm1_g, w_in, dsa_q_g, dsa_k_g, rwkv_mu, rwkv_w0, rwkv_w2,
              rwkv_a0, rwkv_a2, rwkv_g2, rwkv_k_k, rwkv_k_a, rwkv_r_k, rwkv_ln_w, rwkv_ln_b,
              nsa_q_g, nsa_k_g, nsa_pe, nsa_w1, nsa_w2, w_out, norm2_g, ffn_wi, ffn_wo):
    B, T, _ = x.shape
    for l in range(DEPTH):
        mod = jax.nn.silu(c) @ ada_w[l] + ada_b[l]
        sh1, sc1, g1, sh2, sc2, g2 = jnp.split(mod[:, None, :], 6, axis=-1)
        h = rms_norm(x, norm1_g[l]) * (1.0 + sc1) + sh1
        z = h @ w_in[l]
        za, zb, zc = split_cols(z, (sum(A_COLS), sum(B_COLS), sum(C_COLS)))
        qa, ka, va, iq, ik, iw = split_cols(za, A_COLS)
        o_a = dsa_mixer(qa.reshape(B, T, A_HEADS, HEAD_DIM), ka, va,
                        iq.reshape(B, T, IDX_HEADS, IDX_DIM), ik, iw, dsa_q_g[l], dsa_k_g[l])
        o_b = rwkv7_mixer(zb, rwkv_mu[l], rwkv_w0[l], rwkv_w2[l], rwkv_a0[l], rwkv_a2[l], rwkv_g2[l],
                          rwkv_k_k[l], rwkv_k_a[l], rwkv_r_k[l], rwkv_ln_w[l], rwkv_ln_b[l])
        qc, kvc, gc = split_cols(zc, C_COLS)
        kc, vc, ksl, vsl, kwn, vwn = jnp.split(kvc, 6, axis=-1)
        o_c = nsa_mixer(qc.reshape(B, T, C_HEADS, HEAD_DIM), kc, vc, ksl, vsl, kwn, vwn, gc,
                        nsa_q_g[l], nsa_k_g[l], nsa_pe[l], nsa_w1[l], nsa_w2[l])
        mixed = jnp.concatenate([o_a, o_b, o_c], axis=-1) @ w_out[l]
        x = x + g1 * mixed
        h = rms_norm(x, norm2_g[l]) * (1.0 + sc2) + sh2
        gate, up = jnp.split(h @ ffn_wi[l], 2, axis=-1)
        x = x + g2 * ((jax.nn.silu(gate) * up) @ ffn_wo[l])
    return x
```

```python
import functools

import numpy as np
import jax
import jax.numpy as jnp
from jax import lax
from jax.experimental import pallas as pl
from jax.experimental.pallas import tpu as pltpu

F32 = jnp.float32
BF16 = jnp.bfloat16

HEAD_DIM = 64
SLOT = 128
A_HEADS = 4
IDX_HEADS = 4
DSA_TOPK = 256
B_HEADS = 8
B_WIDTH = B_HEADS * HEAD_DIM
DECAY_LORA = 64
AAA_LORA = 64
GATE_LORA = 128
GN_EPS = 64e-5
C_HEADS = 4
CMP_LEN = 32
CMP_STRIDE = 16
CMP_HID = 256
SEL_LEN = 64
SEL_TOPN = 16
WIN = 512
Q_BLOCK = 128
ROPE_THETA = 10000.0
NORM_EPS = 1e-6
NEG = -1e30
INT_MIN = -2147483648
RW_CHUNK = 64
VMEM_LIMIT_BYTES = 56 * 1024 * 1024


def _cparams(*sem):
    return pltpu.CompilerParams(dimension_semantics=sem, vmem_limit_bytes=VMEM_LIMIT_BYTES)


def _dot(a, b):
    return jnp.dot(a, b, preferred_element_type=F32)


def _dot_nt(a, b):
    return lax.dot_general(a, b, (((1,), (1,)), ((), ())), preferred_element_type=F32)


def _split3(x):
    hi = x.astype(BF16)
    r1 = x - hi.astype(F32)
    mid = r1.astype(BF16)
    lo = (r1 - mid.astype(F32)).astype(BF16)
    return hi, mid, lo


def _dot3(x, w):
    hi, mid, lo = _split3(x)
    return _dot(hi, w) + _dot(mid, w) + _dot(lo, w)


def _bmm(a, b):
    return jnp.einsum('hts,hsd->htd', a, b, preferred_element_type=F32)


def _bmm_nt(a, b):
    return jnp.einsum('htj,hsj->hts', a, b, preferred_element_type=F32)


def _bmm_tn(a, b):
    return jnp.einsum('htj,htd->hjd', a, b, preferred_element_type=F32)


def _mod_kernel(c_ref, w_ref, b_ref, o_ref):
    c = c_ref[...]
    s = (c * jax.nn.sigmoid(c)).astype(BF16)
    o_ref[0] = _dot(s, w_ref[0]) + b_ref[0]


def _modulation(c, ada_w, ada_b):
    L, D, N = ada_w.shape
    B = c.shape[0]
    tn = 1536
    return pl.pallas_call(
        _mod_kernel,
        grid=(L, N // tn),
        in_specs=[pl.BlockSpec((B, D), lambda l, j: (0, 0)),
                  pl.BlockSpec((1, D, tn), lambda l, j: (l, 0, j)),
                  pl.BlockSpec((1, 1, tn), lambda l, j: (l, 0, j))],
        out_specs=pl.BlockSpec((1, B, tn), lambda l, j: (l, 0, j)),
        out_shape=jax.ShapeDtypeStruct((L, B, N), F32),
        compiler_params=_cparams("parallel", "parallel"),
        name="adaln_modulation",
    )(c, ada_w.astype(BF16), ada_b.reshape(L, 1, N))


def _in_kernel(x_ref, g_ref, sc_ref, sh_ref, wa_ref, wb_ref, wc_ref, za_ref, zb_ref, zc_ref):
    x = x_ref[0]
    y = x * lax.rsqrt(jnp.mean(x * x, axis=-1, keepdims=True) + NORM_EPS) * g_ref[...]
    h = (y * (1.0 + sc_ref[0]) + sh_ref[0]).astype(BF16)
    za_ref[0] = _dot(h, wa_ref[...])
    zb_ref[0] = _dot(h, wb_ref[...])
    zc_ref[0] = _dot(h, wc_ref[...])


def _in_proj(x, g, sc, sh, wa, wb, wc):
    B, T, D = x.shape
    tm = 256
    na, nb, nc = wa.shape[1], wb.shape[1], wc.shape[1]
    row = lambda b, i: (b, i, 0)
    fix = lambda b, i: (0, 0)
    per_b = lambda b, i: (b, 0, 0)
    return pl.pallas_call(
        _in_kernel,
        grid=(B, T // tm),
        in_specs=[pl.BlockSpec((1, tm, D), row),
                  pl.BlockSpec((1, D), fix),
                  pl.BlockSpec((1, 1, D), per_b),
                  pl.BlockSpec((1, 1, D), per_b),
                  pl.BlockSpec((D, na), fix),
                  pl.BlockSpec((D, nb), fix),
                  pl.BlockSpec((D, nc), fix)],
        out_specs=[pl.BlockSpec((1, tm, na), row),
                   pl.BlockSpec((1, tm, nb), row),
                   pl.BlockSpec((1, tm, nc), row)],
        out_shape=[jax.ShapeDtypeStruct((B, T, na), F32),
                   jax.ShapeDtypeStruct((B, T, nb), F32),
                   jax.ShapeDtypeStruct((B, T, nc), F32)],
        compiler_params=_cparams("parallel", "parallel"),
        name="in_proj",
    )(x, g.reshape(1, D), sc.reshape(B, 1, D), sh.reshape(B, 1, D), wa, wb, wc)


def _norm_rope(x, table):
    rs = lax.rsqrt(jnp.mean(x * x, axis=-1, keepdims=True) + NORM_EPS)
    return x * rs * table


def _fold(y):
    return y + pltpu.roll(y, HEAD_DIM, 1)


def _dsa_prep_kernel(za_ref, cs_ref, gq_ref, gk_ref, qa_ref, iq_ref, ka_ref, ik_ref, va_ref, iw_ref):
    cs = cs_ref[...]
    tq = cs * gq_ref[...] * (HEAD_DIM ** -0.5)
    tk = cs * gk_ref[...]
    for h in range(A_HEADS):
        sl = slice(h * SLOT, (h + 1) * SLOT)
        qa_ref[0, :, sl] = _norm_rope(za_ref[0, :, sl], tq).astype(BF16)
    for h in range(IDX_HEADS):
        sl = slice(h * SLOT, (h + 1) * SLOT)
        src = slice((A_HEADS + h) * SLOT, (A_HEADS + h + 1) * SLOT)
        iq_ref[0, :, sl] = (za_ref[0, :, src] * cs).astype(BF16)
    base = (A_HEADS + IDX_HEADS) * SLOT
    ka_ref[0] = _fold(_norm_rope(za_ref[0, :, base:base + SLOT], tk)).astype(BF16)
    ik_ref[0] = _fold(za_ref[0, :, base + SLOT:base + 2 * SLOT] * cs).astype(BF16)
    va_ref[0] = za_ref[0, :, base + 2 * SLOT:base + 3 * SLOT].astype(BF16)
    iw_ref[0] = za_ref[0, :, base + 3 * SLOT:base + 4 * SLOT]


def _dsa_prep(za, cs, gq, gk):
    B, T, NA = za.shape
    tm = 256
    row = lambda b, i: (b, i, 0)
    fix = lambda b, i: (0, 0)
    outs = [(A_HEADS * SLOT, BF16), (IDX_HEADS * SLOT, BF16), (SLOT, BF16), (SLOT, BF16), (SLOT, BF16),
            (SLOT, F32)]
    return pl.pallas_call(
        _dsa_prep_kernel,
        grid=(B, T // tm),
        in_specs=[pl.BlockSpec((1, tm, NA), row),
                  pl.BlockSpec((tm, SLOT), lambda b, i: (i, 0)),
                  pl.BlockSpec((1, SLOT), fix),
                  pl.BlockSpec((1, SLOT), fix)],
        out_specs=[pl.BlockSpec((1, tm, w), row) for w, _ in outs],
        out_shape=[jax.ShapeDtypeStruct((B, T, w), dt) for w, dt in outs],
        compiler_params=_cparams("parallel", "parallel"),
        name="dsa_prep",
    )(za, cs, gq, gk)


def _masked_attend(q_ref, k, v, mask, o_ref, heads):
    for h in range(heads):
        sl = slice(h * SLOT, (h + 1) * SLOT)
        s = jnp.where(mask, _dot_nt(q_ref[0, :, sl], k), NEG)
        m = jnp.max(s, axis=1, keepdims=True)
        p = jnp.exp(s - m)
        l = jnp.sum(p, axis=1, keepdims=True)
        yield h, _dot(p.astype(BF16), v) / l


def _dsa_kernel(qa_ref, iq_ref, iw_ref, ka_ref, ik_ref, va_ref, o_ref, *, T, topk):
    t0 = pl.program_id(1) * Q_BLOCK
    rows = t0 + lax.broadcasted_iota(jnp.int32, (Q_BLOCK, 1), 0)
    cols = lax.broadcasted_iota(jnp.int32, (1, T), 1)
    causal = cols <= rows
    ik = ik_ref[0]
    iw = iw_ref[0]
    score = jnp.zeros((Q_BLOCK, T), F32)
    for h in range(IDX_HEADS):
        s = _dot_nt(iq_ref[0, :, h * SLOT:(h + 1) * SLOT], ik)
        score = score + iw[:, h:h + 1] * jnp.maximum(s, 0.0)
    score = jnp.where(score == 0.0, 0.0, score)
    bits = lax.bitcast_convert_type(score, jnp.int32)
    key = bits ^ ((bits >> 31) & jnp.int32(0x7FFFFFFF))
    key = jnp.where(causal, key, jnp.int32(INT_MIN))
    kf = float(topk)

    def count_ge(cand):
        return jnp.sum(jnp.where(key >= cand, 1.0, 0.0), axis=1, keepdims=True)

    thr0 = jnp.where(count_ge(jnp.int32(0)) >= kf, jnp.int32(0), jnp.int32(INT_MIN))

    def body(i, thr):
        cand = thr | jnp.left_shift(jnp.int32(1), 30 - i)
        return jnp.where(count_ge(cand) >= kf, cand, thr)

    thr = lax.fori_loop(0, 31, body, thr0)
    gt = key > thr
    eq = key == thr
    need = kf - jnp.sum(jnp.where(gt, 1.0, 0.0), axis=1, keepdims=True)
    ri = lax.broadcasted_iota(jnp.int32, (SLOT, SLOT), 0)
    ci = lax.broadcasted_iota(jnp.int32, (SLOT, SLOT), 1)
    upper = jnp.where(ri <= ci, 1.0, 0.0).astype(BF16)
    run = jnp.zeros((Q_BLOCK, 1), F32)
    sel = []
    for c in range(T // SLOT):
        sl = slice(c * SLOT, (c + 1) * SLOT)
        e = jnp.where(eq[:, sl], 1.0, 0.0)
        incl = _dot(e.astype(BF16), upper)
        sel.append(gt[:, sl] | (eq[:, sl] & (incl - e + run < need)))
        run = run + incl[:, SLOT - 1:SLOT]
    mask = jnp.concatenate(sel, axis=1) & causal
    for h, o in _masked_attend(qa_ref, ka_ref[0], va_ref[0], mask, o_ref, A_HEADS):
        o_ref[0, :, h * SLOT:(h + 1) * SLOT] = o


def _dsa(qa, iq, iw, ka, ik, va):
    B, T, _ = qa.shape
    topk = min(DSA_TOPK, T // 4)
    blk = lambda b, i: (b, i, 0)
    full = lambda b, i: (b, 0, 0)
    return pl.pallas_call(
        functools.partial(_dsa_kernel, T=T, topk=topk),
        grid=(B, T // Q_BLOCK),
        in_specs=[pl.BlockSpec((1, Q_BLOCK, A_HEADS * SLOT), blk),
                  pl.BlockSpec((1, Q_BLOCK, IDX_HEADS * SLOT), blk),
                  pl.BlockSpec((1, Q_BLOCK, SLOT), blk),
                  pl.BlockSpec((1, T, SLOT), full),
                  pl.BlockSpec((1, T, SLOT), full),
                  pl.BlockSpec((1, T, SLOT), full)],
        out_specs=pl.BlockSpec((1, Q_BLOCK, A_HEADS * SLOT), blk),
        out_shape=jax.ShapeDtypeStruct((B, T, A_HEADS * SLOT), F32),
        compiler_params=_cparams("parallel", "parallel"),
        name="dsa_attention",
    )(qa, iq, iw, ka, ik, va)


def _nsa_prep_kernel(zc_ref, cs_ref, gq_ref, gk_ref, qc_ref, ks_ref, kw_ref, vs_ref, vw_ref, gc_ref):
    cs = cs_ref[...]
    tq = cs * gq_ref[...] * (HEAD_DIM ** -0.5)
    tk = cs * gk_ref[...]
    for h in range(C_HEADS):
        sl = slice(h * SLOT, (h + 1) * SLOT)
        qc_ref[0, :, sl] = _norm_rope(zc_ref[0, :, sl], tq).astype(BF16)
    base = C_HEADS * SLOT
    ks_ref[0] = _fold(_norm_rope(zc_ref[0, :, base:base + SLOT], tk)).astype(BF16)
    kw_ref[0] = _fold(_norm_rope(zc_ref[0, :, base + SLOT:base + 2 * SLOT], tk)).astype(BF16)
    vs_ref[0] = zc_ref[0, :, base + 3 * SLOT:base + 4 * SLOT].astype(BF16)
    vw_ref[0] = zc_ref[0, :, base + 4 * SLOT:base + 5 * SLOT].astype(BF16)
    gc_ref[0] = zc_ref[0, :, base + 5 * SLOT:base + 6 * SLOT]


def _nsa_prep(zc, cs, gq, gk):
    B, T, NC = zc.shape
    tm = 256
    row = lambda b, i: (b, i, 0)
    fix = lambda b, i: (0, 0)
    outs = [(C_HEADS * SLOT, BF16), (SLOT, BF16), (SLOT, BF16), (SLOT, BF16), (SLOT, BF16), (SLOT, F32)]
    return pl.pallas_call(
        _nsa_prep_kernel,
        grid=(B, T // tm),
        in_specs=[pl.BlockSpec((1, tm, NC), row),
                  pl.BlockSpec((tm, SLOT), lambda b, i: (i, 0)),
                  pl.BlockSpec((1, SLOT), fix),
                  pl.BlockSpec((1, SLOT), fix)],
        out_specs=[pl.BlockSpec((1, tm, w), row) for w, _ in outs],
        out_shape=[jax.ShapeDtypeStruct((B, T, w), dt) for w, dt in outs],
        compiler_params=_cparams("parallel", "parallel"),
        name="nsa_prep",
    )(zc, cs, gq, gk)


def _nsa_cmp_kernel(rm_ref, pet_ref, peb_ref, w1t_ref, w1b_ref, w2k_ref, w2v_ref, cse_ref, gk_ref,
                    kc_ref, vc_ref):
    rm = rm_ref[0]
    p = _dot((rm + pet_ref[...]).astype(BF16), w1t_ref[...])
    q = _dot((rm + peb_ref[...]).astype(BF16), w1b_ref[...])
    nrow = rm.shape[0]
    hid = jax.nn.gelu(p + pltpu.roll(q, nrow - 1, 0), approximate=True)
    kc = _dot(hid[:, :CMP_HID].astype(BF16), w2k_ref[...])
    kc_ref[0] = _fold(_norm_rope(kc, cse_ref[...] * gk_ref[...])).astype(BF16)
    vc_ref[0] = _dot(hid[:, CMP_HID:].astype(BF16), w2v_ref[...]).astype(BF16)


def _nsa_compress(rm, pet, peb, w1t, w1b, w2k, w2v, cse, gk):
    B, M, W = rm.shape
    fix = lambda b: (0, 0)
    return pl.pallas_call(
        _nsa_cmp_kernel,
        grid=(B,),
        in_specs=[pl.BlockSpec((1, M, W), lambda b: (b, 0, 0)),
                  pl.BlockSpec((1, W), fix), pl.BlockSpec((1, W), fix),
                  pl.BlockSpec(w1t.shape, fix), pl.BlockSpec(w1b.shape, fix),
                  pl.BlockSpec(w2k.shape, fix), pl.BlockSpec(w2v.shape, fix),
                  pl.BlockSpec(cse.shape, fix), pl.BlockSpec((1, SLOT), fix)],
        out_specs=[pl.BlockSpec((1, M, SLOT), lambda b: (b, 0, 0)),
                   pl.BlockSpec((1, M, SLOT), lambda b: (b, 0, 0))],
        out_shape=[jax.ShapeDtypeStruct((B, M, SLOT), BF16), jax.ShapeDtypeStruct((B, M, SLOT), BF16)],
        compiler_params=_cparams("parallel"),
        name="nsa_compress",
    )(rm, pet, peb, w1t, w1b, w2k, w2v, cse, gk)


def _nsa_kernel(qc_ref, gc_ref, kc_ref, vc_ref, ks_ref, vs_ref, kw_ref, vw_ref, ovl_ref, exp_ref, o_ref,
                *, T, n_cmp, n_blk, top_n, win_len):
    t0 = pl.program_id(1) * Q_BLOCK
    rows = t0 + lax.broadcasted_iota(jnp.int32, (Q_BLOCK, 1), 0)
    lane = lax.broadcasted_iota(jnp.int32, (1, SLOT), 1)
    is_cmp = lane < n_cmp
    cvalid = is_cmp & (lane * CMP_STRIDE + (CMP_LEN - 1) <= rows)
    kc = kc_ref[0]
    vc = vc_ref[0]
    psum = jnp.zeros((Q_BLOCK, SLOT), F32)
    o_cmp = []
    for h in range(C_HEADS):
        s = _dot_nt(qc_ref[0, :, h * SLOT:(h + 1) * SLOT], kc)
        s = jnp.where(is_cmp, jnp.where(cvalid, s, NEG), -jnp.inf)
        e = jnp.exp(s - jnp.max(s, axis=1, keepdims=True))
        p = jnp.where(cvalid, e / jnp.sum(e, axis=1, keepdims=True), 0.0)
        psum = psum + p
        o_cmp.append(_dot(p.astype(BF16), vc))
    imp = _dot3(psum, ovl_ref[...])
    cur = rows >> 6
    adm = lane * SEL_LEN <= rows
    forced = (lane == 0) | (lane == cur) | (lane == cur - 1)
    val = jnp.where(adm, jnp.where(forced, jnp.inf, imp), -jnp.inf)
    val = jnp.where(lane < n_blk, val, -jnp.inf)
    rank = jnp.zeros((Q_BLOCK, SLOT), F32)
    for j in range(n_blk):
        c = val[:, j:j + 1]
        rank = rank + jnp.where((c > val) | ((c == val) & (lane > j)), 1.0, 0.0)
    sel = jnp.where((rank < float(top_n)) & (lane < n_blk), 1.0, 0.0).astype(BF16)
    cols = lax.broadcasted_iota(jnp.int32, (1, T), 1)
    mask = (_dot(sel, exp_ref[...]) > 0.5) & (cols <= rows)
    o_slc = [o for _, o in _masked_attend(qc_ref, ks_ref[0], vs_ref[0], mask, o_ref, C_HEADS)]
    ws = pl.multiple_of(jnp.clip(t0 - WIN, 0, T - win_len), Q_BLOCK)
    wpos = ws + lax.broadcasted_iota(jnp.int32, (1, win_len), 1)
    wmask = (wpos <= rows) & (wpos > rows - WIN)
    kw = kw_ref[0, pl.ds(ws, win_len), :]
    vw = vw_ref[0, pl.ds(ws, win_len), :]
    o_win = [o for _, o in _masked_attend(qc_ref, kw, vw, wmask, o_ref, C_HEADS)]
    g = jax.nn.sigmoid(gc_ref[0])
    for h in range(C_HEADS):
        o_ref[0, :, h * SLOT:(h + 1) * SLOT] = (g[:, 3 * h:3 * h + 1] * o_cmp[h]
                                                + g[:, 3 * h + 1:3 * h + 2] * o_slc[h]
                                                + g[:, 3 * h + 2:3 * h + 3] * o_win[h])


def _nsa(qc, gc, kc, vc, ks, vs, kw, vw, ovl, expand, n_cmp):
    B, T, _ = qc.shape
    n_blk = T // SEL_LEN
    blk = lambda b, i: (b, i, 0)
    full = lambda b, i: (b, 0, 0)
    fix = lambda b, i: (0, 0)
    M = kc.shape[1]
    kern = functools.partial(_nsa_kernel, T=T, n_cmp=n_cmp, n_blk=n_blk, top_n=min(SEL_TOPN, n_blk),
                             win_len=min(WIN + Q_BLOCK, T))
    return pl.pallas_call(
        kern,
        grid=(B, T // Q_BLOCK),
        in_specs=[pl.BlockSpec((1, Q_BLOCK, C_HEADS * SLOT), blk),
                  pl.BlockSpec((1, Q_BLOCK, SLOT), blk),
                  pl.BlockSpec((1, M, SLOT), full),
                  pl.BlockSpec((1, M, SLOT), full),
                  pl.BlockSpec((1, T, SLOT), full),
                  pl.BlockSpec((1, T, SLOT), full),
                  pl.BlockSpec((1, T, SLOT), full),
                  pl.BlockSpec((1, T, SLOT), full),
                  pl.BlockSpec(ovl.shape, fix),
                  pl.BlockSpec(expand.shape, fix)],
        out_specs=pl.BlockSpec((1, Q_BLOCK, C_HEADS * SLOT), blk),
        out_shape=jax.ShapeDtypeStruct((B, T, C_HEADS * SLOT), F32),
        compiler_params=_cparams("parallel", "parallel"),
        name="nsa_attention",
    )(qc, gc, kc, vc, ks, vs, kw, vw, ovl, expand)


def _rw_pre_kernel(zb_ref, mu_ref, w0_ref, w2_ref, a0_ref, a2_ref, g2_ref, kk_ref, ka_ref, rk_ref, bd_ref,
                   r_o, k_o, v_o, kk_o, b_o, lw_o, g_o, bon_o, carry):
    @pl.when(pl.program_id(1) == 0)
    def _():
        carry[...] = jnp.zeros_like(carry)

    zt = zb_ref[0]
    tm = zt.shape[0]
    row0 = lax.broadcasted_iota(jnp.int32, (tm, 1), 0) == 0
    prev = jnp.where(row0, carry[7:8, :], pltpu.roll(zt, 1, 0))
    carry[...] = zt[tm - 8:tm, :]
    z = zt + (prev - zt) * mu_ref[...]
    W = B_WIDTH
    r, k, v = z[:, :W], z[:, W:2 * W], z[:, 2 * W:3 * W]
    lora = z[:, 3 * W:3 * W + SLOT]
    gd = z[:, 3 * W + SLOT:3 * W + SLOT + GATE_LORA]
    wl = w0_ref[...] + _dot(jnp.tanh(lora).astype(BF16), w2_ref[...])
    w_log = -jax.nn.softplus(-wl) - 0.5
    lw = -jnp.exp(w_log)
    a = jax.nn.sigmoid(a0_ref[...] + _dot(lora.astype(BF16), a2_ref[...]))
    g = _dot(jax.nn.sigmoid(gd).astype(BF16), g2_ref[...])
    kk = k * kk_ref[...]
    bd = bd_ref[...]
    kk = kk / jnp.maximum(jnp.sqrt(_dot3(kk * kk, bd)), 1e-12)
    k2 = k * (1.0 + (a - 1.0) * ka_ref[...])
    bon = _dot3(r * k2 * rk_ref[...], bd) * v
    outs = ((r_o, r), (k_o, k2), (v_o, v), (kk_o, kk), (b_o, kk * a), (lw_o, lw), (g_o, g), (bon_o, bon))
    for ref, val in outs:
        for h in range(B_HEADS):
            ref[0, h] = val[:, h * HEAD_DIM:(h + 1) * HEAD_DIM]


def _rw_pre(zb, mu, w0, w2e, a0, a2e, g2, k_k, k_a, r_k, bd):
    B, T, NB = zb.shape
    tm = 256
    W = B_WIDTH
    fix = lambda b, i: (0, 0)
    hm = pl.BlockSpec((1, B_HEADS, tm, HEAD_DIM), lambda b, i: (b, 0, i, 0))
    return pl.pallas_call(
        _rw_pre_kernel,
        grid=(B, T // tm),
        in_specs=[pl.BlockSpec((1, tm, NB), lambda b, i: (b, i, 0)),
                  pl.BlockSpec((1, NB), fix),
                  pl.BlockSpec((1, W), fix), pl.BlockSpec((SLOT, W), fix),
                  pl.BlockSpec((1, W), fix), pl.BlockSpec((SLOT, W), fix),
                  pl.BlockSpec((GATE_LORA, W), fix),
                  pl.BlockSpec((1, W), fix), pl.BlockSpec((1, W), fix), pl.BlockSpec((1, W), fix),
                  pl.BlockSpec((W, W), fix)],
        out_specs=[hm] * 8,
        out_shape=[jax.ShapeDtypeStruct((B, B_HEADS, T, HEAD_DIM), F32)] * 8,
        scratch_shapes=[pltpu.VMEM((8, NB), F32)],
        compiler_params=_cparams("parallel", "arbitrary"),
        name="rwkv_prep",
    )(zb, mu, w0, w2e, a0, a2e, g2, k_k, k_a, r_k, bd)


def _rw_chunk_kernel(r_ref, k_ref, v_ref, kk_ref, b_ref, lw_ref, ry_o, yc_o, pm_o, qm_o):
    r, k, v, kk, b, lw = (x[0] for x in (r_ref, k_ref, v_ref, kk_ref, b_ref, lw_ref))
    H, C, N = r.shape
    ti = lax.broadcasted_iota(jnp.int32, (H, C, C), 1)
    si = lax.broadcasted_iota(jnp.int32, (H, C, C), 2)
    tri = jnp.where(si <= ti, 1.0, 0.0).astype(BF16)
    h1, h2, h3 = _split3(lw)
    cum = _bmm(tri, h1) + _bmm(tri, h2) + _bmm(tri, h3)
    tot = cum[:, C - 1:C, :]
    e_in = jnp.exp(cum)
    e_out = jnp.exp(-cum)
    a_s = (kk * jnp.exp(cum - lw)).astype(BF16)
    r_s = r * e_in
    r_b = r_s.astype(BF16)
    b_s = (b * e_out).astype(BF16)
    k_s = (k * e_out).astype(BF16)
    e_end = jnp.exp(tot - cum)
    b_e = (b * e_end).astype(BF16)
    k_e = (k * e_end).astype(BF16)
    vb = v.astype(BF16)
    low = si < ti
    lowi = si <= ti
    l_ab = jnp.where(low, _bmm_nt(a_s, b_s), 0.0)
    l_ak = jnp.where(low, _bmm_nt(a_s, k_s), 0.0).astype(BF16)
    m_rb = jnp.where(lowi, _bmm_nt(r_b, b_s), 0.0).astype(BF16)
    m_rk = jnp.where(lowi, _bmm_nt(r_b, k_s), 0.0).astype(BF16)
    inv = jnp.where(si == ti, 1.0, 0.0)
    m = 1
    while m < C:
        pair = (((ti // m) % 2) == 1) & ((si // m) == (ti // m) - 1)
        lm = jnp.where(pair, l_ab, 0.0).astype(BF16)
        ib = inv.astype(BF16)
        inv = inv - _bmm(_bmm(ib, lm).astype(BF16), ib)
        m *= 2
    ib = inv.astype(BF16)
    w1 = _bmm(ib, a_s).astype(BF16)
    w2 = _bmm(ib, _bmm(l_ak, vb).astype(BF16)).astype(BF16)
    ry_o[0] = r_s - _bmm(m_rb, w1)
    yc_o[0] = _bmm(m_rk, vb) - _bmm(m_rb, w2)
    ji = lax.broadcasted_iota(jnp.int32, (H, N, N), 1)
    di = lax.broadcasted_iota(jnp.int32, (H, N, N), 2)
    decay = jnp.where(ji == di, jnp.broadcast_to(jnp.exp(tot), (H, N, N)), 0.0)
    pm_o[0] = decay - _bmm_tn(b_e, w1)
    qm_o[0] = _bmm_tn(k_e, vb) - _bmm_tn(b_e, w2)


def _rw_chunks(r, k, v, kk, b, lw):
    B, H, T, N = r.shape
    C = RW_CHUNK
    nc = T // C
    tok = pl.BlockSpec((1, H, C, N), lambda bi, c: (bi, 0, c, 0))
    return pl.pallas_call(
        _rw_chunk_kernel,
        grid=(B, nc),
        in_specs=[tok] * 6,
        out_specs=[tok, tok, tok, tok],
        out_shape=[jax.ShapeDtypeStruct((B, H, T, N), F32)] * 2
        + [jax.ShapeDtypeStruct((B, H, nc * N, N), F32)] * 2,
        compiler_params=_cparams("parallel", "parallel"),
        name="rwkv_chunk_ops",
    )(r, k, v, kk, b, lw)


def _rw_scan_kernel(ry_ref, yc_ref, pm_ref, qm_ref, g_ref, bon_ref, lnw_ref, lnb_ref, o_ref, state):
    @pl.when(pl.program_id(1) == 0)
    def _():
        state[...] = jnp.zeros_like(state)

    g0 = state[...]
    gh = g0.astype(BF16)
    gl = (g0 - gh.astype(F32)).astype(BF16)
    ry = ry_ref[0].astype(BF16)
    pm = pm_ref[0].astype(BF16)
    y = _bmm(ry, gh) + _bmm(ry, gl) + yc_ref[0]
    state[...] = _bmm(pm, gh) + _bmm(pm, gl) + qm_ref[0]
    mean = jnp.mean(y, axis=-1, keepdims=True)
    d = y - mean
    var = jnp.mean(d * d, axis=-1, keepdims=True)
    yn = d * lax.rsqrt(var + GN_EPS) * lnw_ref[...] + lnb_ref[...]
    o_ref[0] = (yn + bon_ref[0]) * g_ref[0]


def _rw_scan(ry, yc, pm, qm, g, bon, lnw, lnb):
    B, H, T, N = ry.shape
    C = RW_CHUNK
    tok = pl.BlockSpec((1, H, C, N), lambda bi, c: (bi, 0, c, 0))
    mat = pl.BlockSpec((1, H, N, N), lambda bi, c: (bi, 0, c, 0))
    par = pl.BlockSpec((H, 1, N), lambda bi, c: (0, 0, 0))
    return pl.pallas_call(
        _rw_scan_kernel,
        grid=(B, T // C),
        in_specs=[tok, tok, mat, mat, tok, tok, par, par],
        out_specs=tok,
        out_shape=jax.ShapeDtypeStruct((B, H, T, N), F32),
        scratch_shapes=[pltpu.VMEM((H, N, N), F32)],
        compiler_params=_cparams("parallel", "arbitrary"),
        name="rwkv_scan",
    )(ry, yc, pm, qm, g, bon, lnw, lnb)


def _out_kernel(x_ref, oa_ref, ob_ref, oc_ref, woa_ref, wob_ref, woc_ref, g1_ref, o_ref):
    acc = _dot(oa_ref[0].astype(BF16), woa_ref[...]) + _dot(oc_ref[0].astype(BF16), woc_ref[...])
    for h in range(B_HEADS):
        acc = acc + _dot(ob_ref[0, h].astype(BF16), wob_ref[h])
    o_ref[0] = x_ref[0] + g1_ref[0] * acc


def _out_proj(x, oa, ob, oc, woa, wob, woc, g1):
    B, T, D = x.shape
    tm = 512
    row = lambda b, i: (b, i, 0)
    fix2 = lambda b, i: (0, 0)
    return pl.pallas_call(
        _out_kernel,
        grid=(B, T // tm),
        in_specs=[pl.BlockSpec((1, tm, D), row),
                  pl.BlockSpec((1, tm, oa.shape[2]), row),
                  pl.BlockSpec((1, B_HEADS, tm, HEAD_DIM), lambda b, i: (b, 0, i, 0)),
                  pl.BlockSpec((1, tm, oc.shape[2]), row),
                  pl.BlockSpec(woa.shape, fix2),
                  pl.BlockSpec(wob.shape, lambda b, i: (0, 0, 0)),
                  pl.BlockSpec(woc.shape, fix2),
                  pl.BlockSpec((1, 1, D), lambda b, i: (b, 0, 0))],
        out_specs=pl.BlockSpec((1, tm, D), row),
        out_shape=jax.ShapeDtypeStruct((B, T, D), F32),
        compiler_params=_cparams("parallel", "parallel"),
        name="out_proj",
    )(x, oa, ob, oc, woa, wob, woc, g1.reshape(B, 1, D))


def _ffn_kernel(x_ref, g_ref, sc_ref, sh_ref, g2_ref, wg_ref, wu_ref, wo_ref, o_ref, h_s, acc_s):
    j = pl.program_id(2)

    @pl.when(j == 0)
    def _():
        x = x_ref[0]
        y = x * lax.rsqrt(jnp.mean(x * x, axis=-1, keepdims=True) + NORM_EPS) * g_ref[...]
        h_s[...] = (y * (1.0 + sc_ref[0]) + sh_ref[0]).astype(BF16)
        acc_s[...] = jnp.zeros_like(acc_s)

    h = h_s[...]
    gate = _dot(h, wg_ref[...])
    up = _dot(h, wu_ref[...])
    act = (gate * jax.nn.sigmoid(gate) * up).astype(BF16)
    acc_s[...] += _dot(act, wo_ref[...])

    @pl.when(j == pl.num_programs(2) - 1)
    def _():
        o_ref[0] = x_ref[0] + g2_ref[0] * acc_s[...]


def _ffn(x, g, sc, sh, g2, wi, wo):
    B, T, D = x.shape
    F = wo.shape[0]
    tm, th = 1024, 256
    nh = F // th
    row = lambda b, i, j: (b, i, 0)
    per_b = lambda b, i, j: (b, 0, 0)
    return pl.pallas_call(
        _ffn_kernel,
        grid=(B, T // tm, nh),
        in_specs=[pl.BlockSpec((1, tm, D), row),
                  pl.BlockSpec((1, D), lambda b, i, j: (0, 0)),
                  pl.BlockSpec((1, 1, D), per_b),
                  pl.BlockSpec((1, 1, D), per_b),
                  pl.BlockSpec((1, 1, D), per_b),
                  pl.BlockSpec((D, th), lambda b, i, j: (0, j)),
                  pl.BlockSpec((D, th), lambda b, i, j: (0, j + nh)),
                  pl.BlockSpec((th, D), lambda b, i, j: (j, 0))],
        out_specs=pl.BlockSpec((1, tm, D), row),
        out_shape=jax.ShapeDtypeStruct((B, T, D), F32),
        scratch_shapes=[pltpu.VMEM((tm, D), BF16), pltpu.VMEM((tm, D), F32)],
        compiler_params=_cparams("parallel", "parallel", "arbitrary"),
        name="ffn_swiglu",
    )(x, g.reshape(1, D), sc.reshape(B, 1, D), sh.reshape(B, 1, D), g2.reshape(B, 1, D), wi, wi, wo)


def _rot_cols(w):
    half = HEAD_DIM // 2
    return jnp.concatenate([-w[..., half:], w[..., :half]], axis=-1)


def _rope_slot(w):
    return jnp.concatenate([w, _rot_cols(w)], axis=-1)


def _pad_slot(w):
    return jnp.pad(w, [(0, 0)] * (w.ndim - 1) + [(0, SLOT - w.shape[-1])])


def _perm_gain(g):
    half = HEAD_DIM // 2
    return jnp.concatenate([g, g[half:], g[:half]]).reshape(1, SLOT)


def _rope_table(pos):
    half = HEAD_DIM // 2
    inv = ROPE_THETA ** (-np.arange(half, dtype=np.float64) / half)
    ang = np.asarray(pos, np.float64)[:, None] * inv[None, :]
    cos, sin = np.cos(ang), np.sin(ang)
    return jnp.asarray(np.concatenate([cos, cos, sin, sin], axis=1), F32)


def _in_weights(w):
    D = w.shape[0]
    hd = HEAD_DIM
    o = 0
    qa = w[:, o:o + A_HEADS * hd]; o += A_HEADS * hd
    ka = w[:, o:o + hd]; o += hd
    va = w[:, o:o + hd]; o += hd
    iq = w[:, o:o + IDX_HEADS * hd]; o += IDX_HEADS * hd
    ik = w[:, o:o + hd]; o += hd
    iw = w[:, o:o + IDX_HEADS]; o += IDX_HEADS
    nb = 3 * B_WIDTH + DECAY_LORA + AAA_LORA + GATE_LORA
    wb = w[:, o:o + nb]; o += nb
    qc = w[:, o:o + C_HEADS * hd]; o += C_HEADS * hd
    kc, vc, ksl, vsl, kwn, vwn = (w[:, o + i * hd:o + (i + 1) * hd] for i in range(6)); o += 6 * hd
    gc = w[:, o:o + 3 * C_HEADS]
    heads = lambda m, n: [m[:, i * hd:(i + 1) * hd] for i in range(n)]
    wa = jnp.concatenate([_rope_slot(h) for h in heads(qa, A_HEADS)]
                         + [_rope_slot(h) for h in heads(iq, IDX_HEADS)]
                         + [_rope_slot(ka), _rope_slot(ik), _pad_slot(va), _pad_slot(iw)], axis=1)
    wc = jnp.concatenate([_rope_slot(h) for h in heads(qc, C_HEADS)]
                         + [_rope_slot(ksl), _rope_slot(kwn), jnp.concatenate([kc, vc], axis=1),
                            _pad_slot(vsl), _pad_slot(vwn), _pad_slot(gc)], axis=1)
    return wa.astype(BF16), wb.astype(BF16), wc.astype(BF16)


def _cmp_weights(pe, w1, w2):
    half = CMP_LEN // 2
    hd = HEAD_DIM
    zero = jnp.zeros((half, hd, CMP_HID), F32)

    def expand(lo):
        wk = w1[0].reshape(CMP_LEN, hd, CMP_HID)[lo:lo + half]
        wv = w1[1].reshape(CMP_LEN, hd, CMP_HID)[lo:lo + half]
        k_rows = jnp.concatenate([wk, zero], axis=1)
        v_rows = jnp.concatenate([zero, wv], axis=1)
        return jnp.concatenate([k_rows, v_rows], axis=2).reshape(half * SLOT, 2 * CMP_HID).astype(BF16)

    def pe_row(lo):
        return jnp.concatenate([pe[0, lo:lo + half], pe[1, lo:lo + half]], axis=1).reshape(1, half * SLOT)

    w2k = _rope_slot(w2[0]).astype(BF16)
    w2v = _pad_slot(w2[1]).astype(BF16)
    return pe_row(0), pe_row(half), expand(0), expand(half), w2k, w2v


def _nsa_tables(T):
    n_cmp = (T - CMP_LEN) // CMP_STRIDE + 1
    n_blk = T // SEL_LEN
    starts = np.arange(n_cmp) * CMP_STRIDE
    end_pos = starts + CMP_LEN - 1
    sel_start = np.arange(n_blk) * SEL_LEN
    ovl = np.zeros((SLOT, SLOT), np.float32)
    ovl[:n_cmp, :n_blk] = ((starts[:, None] <= sel_start[None, :] + SEL_LEN - 1)
                           & (end_pos[:, None] >= sel_start[None, :]))
    expand = np.zeros((SLOT, T), np.float32)
    expand[np.arange(T) // SEL_LEN, np.arange(T)] = 1.0
    m = T // CMP_STRIDE
    cse = _rope_table(np.arange(m) * CMP_STRIDE + CMP_LEN - 1)
    return n_cmp, jnp.asarray(ovl, BF16), jnp.asarray(expand, BF16), cse


def _out_weights(w):
    hd = HEAD_DIM
    D = w.shape[1]

    def slots(base, n):
        blocks = [jnp.concatenate([w[base + i * hd:base + (i + 1) * hd], jnp.zeros((SLOT - hd, D), F32)])
                  for i in range(n)]
        return jnp.concatenate(blocks).astype(BF16)

    a0 = 0
    b0 = A_HEADS * hd
    c0 = b0 + B_WIDTH
    return slots(a0, A_HEADS), w[b0:c0].reshape(B_HEADS, hd, D).astype(BF16), slots(c0, C_HEADS)


def kernel(x, c, ada_w, ada_b, norm1_g, w_in, dsa_q_g, dsa_k_g, rwkv_mu, rwkv_w0, rwkv_w2, rwkv_a0, rwkv_a2, rwkv_g2, rwkv_k_k, rwkv_k_a, rwkv_r_k, rwkv_ln_w, rwkv_ln_b, nsa_q_g, nsa_k_g, nsa_pe, nsa_w1, nsa_w2, w_out, norm2_g, ffn_wi, ffn_wo):
    B, T, D = x.shape
    L = w_in.shape[0]
    W = B_WIDTH
    assert T % 256 == 0 and T % RW_CHUNK == 0 and D % SLOT == 0
    mod = _modulation(c, ada_w, ada_b)
    cs = _rope_table(np.arange(T))
    n_cmp, ovl, expand, cse = _nsa_tables(T)
    bd = jnp.asarray(np.kron(np.eye(B_HEADS), np.ones((HEAD_DIM, HEAD_DIM))), BF16)
    lora_pad = jnp.zeros((SLOT - DECAY_LORA, W), F32)
    for l in range(L):
        sh1, sc1, g1, sh2, sc2, g2 = (mod[l, :, i * D:(i + 1) * D] for i in range(6))
        wa, wb, wc = _in_weights(w_in[l])
        za, zb, zc = _in_proj(x, norm1_g[l], sc1, sh1, wa, wb, wc)
        qa, iq, ka, ik, va, iw = _dsa_prep(za, cs, _perm_gain(dsa_q_g[l]), _perm_gain(dsa_k_g[l]))
        oa = _dsa(qa, iq, iw, ka, ik, va)
        w2e = jnp.concatenate([rwkv_w2[l], lora_pad]).astype(BF16)
        a2e = jnp.concatenate([lora_pad, rwkv_a2[l]]).astype(BF16)
        row = lambda p: p.reshape(1, W)
        r, k2, v, kk, bb, lw, gg, bon = _rw_pre(
            zb, rwkv_mu[l].reshape(1, -1), row(rwkv_w0[l]), w2e, row(rwkv_a0[l]), a2e,
            rwkv_g2[l].astype(BF16), row(rwkv_k_k[l]), row(rwkv_k_a[l]), row(rwkv_r_k[l]), bd)
        ry, yc, pm, qm = _rw_chunks(r, k2, v, kk, bb, lw)
        ob = _rw_scan(ry, yc, pm, qm, gg, bon, rwkv_ln_w[l].reshape(B_HEADS, 1, HEAD_DIM),
                      rwkv_ln_b[l].reshape(B_HEADS, 1, HEAD_DIM))
        gkc = _perm_gain(nsa_k_g[l])
        qc, ks, kw, vs, vw, gc = _nsa_prep(zc, cs, _perm_gain(nsa_q_g[l]), gkc)
        kv_slot = (C_HEADS + 2) * SLOT
        rm = zc[:, :, kv_slot:kv_slot + SLOT].reshape(B, T // CMP_STRIDE, CMP_STRIDE * SLOT)
        kc, vc = _nsa_compress(rm, *_cmp_weights(nsa_pe[l], nsa_w1[l], nsa_w2[l]), cse, gkc)
        oc = _nsa(qc, gc, kc, vc, ks, vs, kw, vw, ovl, expand, n_cmp)
        woa, wob, woc = _out_weights(w_out[l])
        x = _out_proj(x, oa, ob, oc, woa, wob, woc, g1)
        x = _ffn(x, norm2_g[l], sc2, sh2, g2, ffn_wi[l].astype(BF16), ffn_wo[l].astype(BF16))
    return x
```

```python
import functools

import numpy as np
import jax
import jax.numpy as jnp
from jax import lax
from jax.experimental import pallas as pl
from jax.experimental.pallas import tpu as pltpu

F32 = jnp.float32
BF16 = jnp.bfloat16

HEAD_DIM = 64
SLOT = 128
A_HEADS = 4
IDX_HEADS = 4
DSA_TOPK = 256
B_HEADS = 8
B_WIDTH = B_HEADS * HEAD_DIM
DECAY_LORA = 64
AAA_LORA = 64
GATE_LORA = 128
GN_EPS = 64e-5
C_HEADS = 4
CMP_LEN = 32
CMP_STRIDE = 16
CMP_HID = 256
SEL_LEN = 64
SEL_TOPN = 16
WIN = 512
Q_BLOCK = 128
ROPE_THETA = 10000.0
NORM_EPS = 1e-6
NEG = -1e30
INT_MIN = -2147483648
RW_CHUNK = 64
PREFIX_GROUPS = 8
VMEM_LIMIT_BYTES = 56 * 1024 * 1024


def _cparams(*sem):
    return pltpu.CompilerParams(dimension_semantics=sem, vmem_limit_bytes=VMEM_LIMIT_BYTES)


def _dot(a, b):
    return jnp.dot(a, b, preferred_element_type=F32)


def _dot_nt(a, b):
    return lax.dot_general(a, b, (((1,), (1,)), ((), ())), preferred_element_type=F32)


def _split3(x):
    hi = x.astype(BF16)
    r1 = x - hi.astype(F32)
    mid = r1.astype(BF16)
    lo = (r1 - mid.astype(F32)).astype(BF16)
    return hi, mid, lo


def _dot3(x, w):
    hi, mid, lo = _split3(x)
    return _dot(hi, w) + _dot(mid, w) + _dot(lo, w)


def _bmm(a, b):
    return jnp.einsum('hts,hsd->htd', a, b, preferred_element_type=F32)


def _bmm_nt(a, b):
    return jnp.einsum('htj,hsj->hts', a, b, preferred_element_type=F32)


def _bmm_tn(a, b):
    return jnp.einsum('htj,htd->hjd', a, b, preferred_element_type=F32)


def _mod_kernel(c_ref, w_ref, b_ref, o_ref):
    c = c_ref[...]
    s = (c * jax.nn.sigmoid(c)).astype(BF16)
    o_ref[0] = _dot(s, w_ref[0]) + b_ref[0]


def _modulation(c, ada_w, ada_b):
    L, D, N = ada_w.shape
    B = c.shape[0]
    tn = 1536
    return pl.pallas_call(
        _mod_kernel,
        grid=(L, N // tn),
        in_specs=[pl.BlockSpec((B, D), lambda l, j: (0, 0)),
                  pl.BlockSpec((1, D, tn), lambda l, j: (l, 0, j)),
                  pl.BlockSpec((1, 1, tn), lambda l, j: (l, 0, j))],
        out_specs=pl.BlockSpec((1, B, tn), lambda l, j: (l, 0, j)),
        out_shape=jax.ShapeDtypeStruct((L, B, N), F32),
        compiler_params=_cparams("parallel", "parallel"),
        name="adaln_modulation",
    )(c, ada_w.astype(BF16), ada_b.reshape(L, 1, N))


def _in_kernel(x_ref, g_ref, sc_ref, sh_ref, wa_ref, wb_ref, wc_ref, za_ref, zb_ref, zc_ref):
    x = x_ref[0]
    y = x * lax.rsqrt(jnp.mean(x * x, axis=-1, keepdims=True) + NORM_EPS) * g_ref[...]
    h = (y * (1.0 + sc_ref[0]) + sh_ref[0]).astype(BF16)
    za_ref[0] = _dot(h, wa_ref[...])
    zb_ref[0] = _dot(h, wb_ref[...])
    zc_ref[0] = _dot(h, wc_ref[...])


def _in_proj(x, g, sc, sh, wa, wb, wc):
    B, T, D = x.shape
    tm = 256
    na, nb, nc = wa.shape[1], wb.shape[1], wc.shape[1]
    row = lambda b, i: (b, i, 0)
    fix = lambda b, i: (0, 0)
    per_b = lambda b, i: (b, 0, 0)
    return pl.pallas_call(
        _in_kernel,
        grid=(B, T // tm),
        in_specs=[pl.BlockSpec((1, tm, D), row),
                  pl.BlockSpec((1, D), fix),
                  pl.BlockSpec((1, 1, D), per_b),
                  pl.BlockSpec((1, 1, D), per_b),
                  pl.BlockSpec((D, na), fix),
                  pl.BlockSpec((D, nb), fix),
                  pl.BlockSpec((D, nc), fix)],
        out_specs=[pl.BlockSpec((1, tm, na), row),
                   pl.BlockSpec((1, tm, nb), row),
                   pl.BlockSpec((1, tm, nc), row)],
        out_shape=[jax.ShapeDtypeStruct((B, T, na), F32),
                   jax.ShapeDtypeStruct((B, T, nb), F32),
                   jax.ShapeDtypeStruct((B, T, nc), F32)],
        compiler_params=_cparams("parallel", "parallel"),
        name="in_proj",
    )(x, g.reshape(1, D), sc.reshape(B, 1, D), sh.reshape(B, 1, D), wa, wb, wc)


def _norm_rope(x, table):
    rs = lax.rsqrt(jnp.mean(x * x, axis=-1, keepdims=True) + NORM_EPS)
    return x * rs * table


def _fold(y):
    return y + pltpu.roll(y, HEAD_DIM, 1)


def _dsa_prep_kernel(za_ref, cs_ref, gq_ref, gk_ref, qa_ref, iq_ref, ka_ref, ik_ref, va_ref, iw_ref):
    cs = cs_ref[...]
    tq = cs * gq_ref[...] * (HEAD_DIM ** -0.5)
    tk = cs * gk_ref[...]
    for h in range(A_HEADS):
        sl = slice(h * SLOT, (h + 1) * SLOT)
        qa_ref[0, :, sl] = _norm_rope(za_ref[0, :, sl], tq).astype(BF16)
    for h in range(IDX_HEADS):
        sl = slice(h * SLOT, (h + 1) * SLOT)
        src = slice((A_HEADS + h) * SLOT, (A_HEADS + h + 1) * SLOT)
        iq_ref[0, :, sl] = (za_ref[0, :, src] * cs).astype(BF16)
    base = (A_HEADS + IDX_HEADS) * SLOT
    ka_ref[0] = _fold(_norm_rope(za_ref[0, :, base:base + SLOT], tk)).astype(BF16)
    ik_ref[0] = _fold(za_ref[0, :, base + SLOT:base + 2 * SLOT] * cs).astype(BF16)
    va_ref[0] = za_ref[0, :, base + 2 * SLOT:base + 3 * SLOT].astype(BF16)
    iw_ref[0] = za_ref[0, :, base + 3 * SLOT:base + 4 * SLOT]


def _dsa_prep(za, cs, gq, gk):
    B, T, NA = za.shape
    tm = 256
    row = lambda b, i: (b, i, 0)
    fix = lambda b, i: (0, 0)
    outs = [(A_HEADS * SLOT, BF16), (IDX_HEADS * SLOT, BF16), (SLOT, BF16), (SLOT, BF16), (SLOT, BF16),
            (SLOT, F32)]
    return pl.pallas_call(
        _dsa_prep_kernel,
        grid=(B, T // tm),
        in_specs=[pl.BlockSpec((1, tm, NA), row),
                  pl.BlockSpec((tm, SLOT), lambda b, i: (i, 0)),
                  pl.BlockSpec((1, SLOT), fix),
                  pl.BlockSpec((1, SLOT), fix)],
        out_specs=[pl.BlockSpec((1, tm, w), row) for w, _ in outs],
        out_shape=[jax.ShapeDtypeStruct((B, T, w), dt) for w, dt in outs],
        compiler_params=_cparams("parallel", "parallel"),
        name="dsa_prep",
    )(za, cs, gq, gk)


def _masked_attend(q_ref, k, v, mask, heads):
    for h in range(heads):
        sl = slice(h * SLOT, (h + 1) * SLOT)
        s = jnp.where(mask, _dot_nt(q_ref[0, :, sl], k), NEG)
        m = jnp.max(s, axis=1, keepdims=True)
        p = jnp.exp(s - m)
        l = jnp.sum(p, axis=1, keepdims=True)
        yield h, _dot(p.astype(BF16), v) / l


def _dsa_block(qa_ref, iq_ref, iw_ref, ka_ref, ik_ref, va_ref, o_ref, t0, Te, topk):
    rows = t0 + lax.broadcasted_iota(jnp.int32, (Q_BLOCK, 1), 0)
    cols = lax.broadcasted_iota(jnp.int32, (1, Te), 1)
    causal = cols <= rows
    if Te <= topk:
        mask = causal
    else:
        ik = ik_ref[0, :Te, :]
        iw = iw_ref[0]
        score = jnp.zeros((Q_BLOCK, Te), F32)
        for h in range(IDX_HEADS):
            s = _dot_nt(iq_ref[0, :, h * SLOT:(h + 1) * SLOT], ik)
            score = score + iw[:, h:h + 1] * jnp.maximum(s, 0.0)
        score = jnp.where(score == 0.0, 0.0, score)
        bits = lax.bitcast_convert_type(score, jnp.int32)
        key = bits ^ ((bits >> 31) & jnp.int32(0x7FFFFFFF))
        key = jnp.where(causal, key, jnp.int32(INT_MIN))
        kf = float(topk)

        def count_ge(cand):
            return jnp.sum(jnp.where(key >= cand, 1.0, 0.0), axis=1, keepdims=True)

        thr0 = jnp.where(count_ge(jnp.int32(0)) >= kf, jnp.int32(0), jnp.int32(INT_MIN))

        def body(i, thr):
            cand = thr | jnp.left_shift(jnp.int32(1), 30 - i)
            return jnp.where(count_ge(cand) >= kf, cand, thr)

        thr = lax.fori_loop(0, 31, body, thr0)
        gt = key > thr
        eq = key == thr
        need = kf - jnp.sum(jnp.where(gt, 1.0, 0.0), axis=1, keepdims=True)
        ri = lax.broadcasted_iota(jnp.int32, (SLOT, SLOT), 0)
        ci = lax.broadcasted_iota(jnp.int32, (SLOT, SLOT), 1)
        upper = jnp.where(ri <= ci, 1.0, 0.0).astype(BF16)
        run = jnp.zeros((Q_BLOCK, 1), F32)
        sel = []
        for c in range(Te // SLOT):
            sl = slice(c * SLOT, (c + 1) * SLOT)
            e = jnp.where(eq[:, sl], 1.0, 0.0)
            incl = _dot(e.astype(BF16), upper)
            sel.append(gt[:, sl] | (eq[:, sl] & (incl - e + run < need)))
            run = run + incl[:, SLOT - 1:SLOT]
        mask = jnp.concatenate(sel, axis=1) & causal
    for h, o in _masked_attend(qa_ref, ka_ref[0, :Te, :], va_ref[0, :Te, :], mask, A_HEADS):
        o_ref[0, :, h * SLOT:(h + 1) * SLOT] = o


def _dsa_kernel(qa_ref, iq_ref, iw_ref, ka_ref, ik_ref, va_ref, o_ref, *, T, topk, groups):
    qi = pl.program_id(1)
    per = (T // Q_BLOCK) // groups
    for g in range(groups):
        @pl.when(qi // per == g)
        def _(g=g):
            _dsa_block(qa_ref, iq_ref, iw_ref, ka_ref, ik_ref, va_ref, o_ref, qi * Q_BLOCK,
                       (g + 1) * per * Q_BLOCK, topk)


def _dsa(qa, iq, iw, ka, ik, va):
    B, T, _ = qa.shape
    topk = min(DSA_TOPK, T // 4)
    blk = lambda b, i: (b, i, 0)
    full = lambda b, i: (b, 0, 0)
    return pl.pallas_call(
        functools.partial(_dsa_kernel, T=T, topk=topk, groups=PREFIX_GROUPS),
        grid=(B, T // Q_BLOCK),
        in_specs=[pl.BlockSpec((1, Q_BLOCK, A_HEADS * SLOT), blk),
                  pl.BlockSpec((1, Q_BLOCK, IDX_HEADS * SLOT), blk),
                  pl.BlockSpec((1, Q_BLOCK, SLOT), blk),
                  pl.BlockSpec((1, T, SLOT), full),
                  pl.BlockSpec((1, T, SLOT), full),
                  pl.BlockSpec((1, T, SLOT), full)],
        out_specs=pl.BlockSpec((1, Q_BLOCK, A_HEADS * SLOT), blk),
        out_shape=jax.ShapeDtypeStruct((B, T, A_HEADS * SLOT), F32),
        compiler_params=_cparams("parallel", "parallel"),
        name="dsa_attention",
    )(qa, iq, iw, ka, ik, va)


def _nsa_prep_kernel(zc_ref, cs_ref, gq_ref, gk_ref, qc_ref, ks_ref, kw_ref, vs_ref, vw_ref, gc_ref):
    cs = cs_ref[...]
    tq = cs * gq_ref[...] * (HEAD_DIM ** -0.5)
    tk = cs * gk_ref[...]
    for h in range(C_HEADS):
        sl = slice(h * SLOT, (h + 1) * SLOT)
        qc_ref[0, :, sl] = _norm_rope(zc_ref[0, :, sl], tq).astype(BF16)
    base = C_HEADS * SLOT
    ks_ref[0] = _fold(_norm_rope(zc_ref[0, :, base:base + SLOT], tk)).astype(BF16)
    kw_ref[0] = _fold(_norm_rope(zc_ref[0, :, base + SLOT:base + 2 * SLOT], tk)).astype(BF16)
    vs_ref[0] = zc_ref[0, :, base + 3 * SLOT:base + 4 * SLOT].astype(BF16)
    vw_ref[0] = zc_ref[0, :, base + 4 * SLOT:base + 5 * SLOT].astype(BF16)
    gc_ref[0] = zc_ref[0, :, base + 5 * SLOT:base + 6 * SLOT]


def _nsa_prep(zc, cs, gq, gk):
    B, T, NC = zc.shape
    tm = 256
    row = lambda b, i: (b, i, 0)
    fix = lambda b, i: (0, 0)
    outs = [(C_HEADS * SLOT, BF16), (SLOT, BF16), (SLOT, BF16), (SLOT, BF16), (SLOT, BF16), (SLOT, F32)]
    return pl.pallas_call(
        _nsa_prep_kernel,
        grid=(B, T // tm),
        in_specs=[pl.BlockSpec((1, tm, NC), row),
                  pl.BlockSpec((tm, SLOT), lambda b, i: (i, 0)),
                  pl.BlockSpec((1, SLOT), fix),
                  pl.BlockSpec((1, SLOT), fix)],
        out_specs=[pl.BlockSpec((1, tm, w), row) for w, _ in outs],
        out_shape=[jax.ShapeDtypeStruct((B, T, w), dt) for w, dt in outs],
        compiler_params=_cparams("parallel", "parallel"),
        name="nsa_prep",
    )(zc, cs, gq, gk)


def _nsa_cmp_kernel(rm_ref, pet_ref, peb_ref, w1t_ref, w1b_ref, w2k_ref, w2v_ref, cse_ref, gk_ref,
                    kc_ref, vc_ref):
    rm = rm_ref[0]
    p = _dot((rm + pet_ref[...]).astype(BF16), w1t_ref[...])
    q = _dot((rm + peb_ref[...]).astype(BF16), w1b_ref[...])
    nrow = rm.shape[0]
    hid = jax.nn.gelu(p + pltpu.roll(q, nrow - 1, 0), approximate=True)
    kc = _dot(hid[:, :CMP_HID].astype(BF16), w2k_ref[...])
    kc_ref[0] = _fold(_norm_rope(kc, cse_ref[...] * gk_ref[...])).astype(BF16)
    vc_ref[0] = _dot(hid[:, CMP_HID:].astype(BF16), w2v_ref[...]).astype(BF16)


def _nsa_compress(rm, pet, peb, w1t, w1b, w2k, w2v, cse, gk):
    B, M, W = rm.shape
    fix = lambda b: (0, 0)
    return pl.pallas_call(
        _nsa_cmp_kernel,
        grid=(B,),
        in_specs=[pl.BlockSpec((1, M, W), lambda b: (b, 0, 0)),
                  pl.BlockSpec((1, W), fix), pl.BlockSpec((1, W), fix),
                  pl.BlockSpec(w1t.shape, fix), pl.BlockSpec(w1b.shape, fix),
                  pl.BlockSpec(w2k.shape, fix), pl.BlockSpec(w2v.shape, fix),
                  pl.BlockSpec(cse.shape, fix), pl.BlockSpec((1, SLOT), fix)],
        out_specs=[pl.BlockSpec((1, M, SLOT), lambda b: (b, 0, 0)),
                   pl.BlockSpec((1, M, SLOT), lambda b: (b, 0, 0))],
        out_shape=[jax.ShapeDtypeStruct((B, M, SLOT), BF16), jax.ShapeDtypeStruct((B, M, SLOT), BF16)],
        compiler_params=_cparams("parallel"),
        name="nsa_compress",
    )(rm, pet, peb, w1t, w1b, w2k, w2v, cse, gk)


def _nsa_kernel(qc_ref, gc_ref, kc_ref, vc_ref, ks_ref, vs_ref, kw_ref, vw_ref, ovl_ref, exp_ref, o_ref,
                slc_s, *, T, n_cmp, n_blk, top_n, win_len, groups):
    t0 = pl.program_id(1) * Q_BLOCK
    rows = t0 + lax.broadcasted_iota(jnp.int32, (Q_BLOCK, 1), 0)
    lane = lax.broadcasted_iota(jnp.int32, (1, SLOT), 1)
    is_cmp = lane < n_cmp
    cvalid = is_cmp & (lane * CMP_STRIDE + (CMP_LEN - 1) <= rows)
    kc = kc_ref[0]
    vc = vc_ref[0]
    psum = jnp.zeros((Q_BLOCK, SLOT), F32)
    o_cmp = []
    for h in range(C_HEADS):
        s = _dot_nt(qc_ref[0, :, h * SLOT:(h + 1) * SLOT], kc)
        s = jnp.where(is_cmp, jnp.where(cvalid, s, NEG), -jnp.inf)
        e = jnp.exp(s - jnp.max(s, axis=1, keepdims=True))
        p = jnp.where(cvalid, e / jnp.sum(e, axis=1, keepdims=True), 0.0)
        psum = psum + p
        o_cmp.append(_dot(p.astype(BF16), vc))
    imp = _dot3(psum, ovl_ref[...])
    cur = rows >> 6
    adm = lane * SEL_LEN <= rows
    forced = (lane == 0) | (lane == cur) | (lane == cur - 1)
    val = jnp.where(adm, jnp.where(forced, jnp.inf, imp), -jnp.inf)
    val = jnp.where(lane < n_blk, val, -jnp.inf)
    rank = jnp.zeros((Q_BLOCK, SLOT), F32)
    for j in range(n_blk):
        c = val[:, j:j + 1]
        rank = rank + jnp.where((c > val) | ((c == val) & (lane > j)), 1.0, 0.0)
    sel = jnp.where((rank < float(top_n)) & (lane < n_blk), 1.0, 0.0).astype(BF16)
    qi = pl.program_id(1)
    per = (T // Q_BLOCK) // groups
    for gi in range(groups):
        @pl.when(qi // per == gi)
        def _(gi=gi):
            te = (gi + 1) * per * Q_BLOCK
            cols = lax.broadcasted_iota(jnp.int32, (1, te), 1)
            mask = (_dot(sel, exp_ref[:, :te]) > 0.5) & (cols <= rows)
            for h, o in _masked_attend(qc_ref, ks_ref[0, :te, :], vs_ref[0, :te, :], mask, C_HEADS):
                slc_s[:, h * SLOT:(h + 1) * SLOT] = o
    ws = pl.multiple_of(jnp.clip(t0 - WIN, 0, T - win_len), Q_BLOCK)
    wpos = ws + lax.broadcasted_iota(jnp.int32, (1, win_len), 1)
    wmask = (wpos <= rows) & (wpos > rows - WIN)
    kw = kw_ref[0, pl.ds(ws, win_len), :]
    vw = vw_ref[0, pl.ds(ws, win_len), :]
    o_win = [o for _, o in _masked_attend(qc_ref, kw, vw, wmask, C_HEADS)]
    g = jax.nn.sigmoid(gc_ref[0])
    for h in range(C_HEADS):
        o_ref[0, :, h * SLOT:(h + 1) * SLOT] = (g[:, 3 * h:3 * h + 1] * o_cmp[h]
                                                + g[:, 3 * h + 1:3 * h + 2] * slc_s[:, h * SLOT:(h + 1) * SLOT]
                                                + g[:, 3 * h + 2:3 * h + 3] * o_win[h])


def _nsa(qc, gc, kc, vc, ks, vs, kw, vw, ovl, expand, n_cmp):
    B, T, _ = qc.shape
    n_blk = T // SEL_LEN
    blk = lambda b, i: (b, i, 0)
    full = lambda b, i: (b, 0, 0)
    fix = lambda b, i: (0, 0)
    M = kc.shape[1]
    kern = functools.partial(_nsa_kernel, T=T, n_cmp=n_cmp, n_blk=n_blk, top_n=min(SEL_TOPN, n_blk),
                             win_len=min(WIN + Q_BLOCK, T), groups=PREFIX_GROUPS)
    return pl.pallas_call(
        kern,
        grid=(B, T // Q_BLOCK),
        in_specs=[pl.BlockSpec((1, Q_BLOCK, C_HEADS * SLOT), blk),
                  pl.BlockSpec((1, Q_BLOCK, SLOT), blk),
                  pl.BlockSpec((1, M, SLOT), full),
                  pl.BlockSpec((1, M, SLOT), full),
                  pl.BlockSpec((1, T, SLOT), full),
                  pl.BlockSpec((1, T, SLOT), full),
                  pl.BlockSpec((1, T, SLOT), full),
                  pl.BlockSpec((1, T, SLOT), full),
                  pl.BlockSpec(ovl.shape, fix),
                  pl.BlockSpec(expand.shape, fix)],
        out_specs=pl.BlockSpec((1, Q_BLOCK, C_HEADS * SLOT), blk),
        out_shape=jax.ShapeDtypeStruct((B, T, C_HEADS * SLOT), F32),
        scratch_shapes=[pltpu.VMEM((Q_BLOCK, C_HEADS * SLOT), F32)],
        compiler_params=_cparams("parallel", "parallel"),
        name="nsa_attention",
    )(qc, gc, kc, vc, ks, vs, kw, vw, ovl, expand)


def _rw_pre_kernel(zb_ref, mu_ref, w0_ref, w2_ref, a0_ref, a2_ref, g2_ref, kk_ref, ka_ref, rk_ref, bd_ref,
                   r_o, k_o, v_o, kk_o, b_o, lw_o, g_o, bon_o, carry):
    @pl.when(pl.program_id(1) == 0)
    def _():
        carry[...] = jnp.zeros_like(carry)

    zt = zb_ref[0]
    tm = zt.shape[0]
    row0 = lax.broadcasted_iota(jnp.int32, (tm, 1), 0) == 0
    prev = jnp.where(row0, carry[7:8, :], pltpu.roll(zt, 1, 0))
    carry[...] = zt[tm - 8:tm, :]
    z = zt + (prev - zt) * mu_ref[...]
    W = B_WIDTH
    r, k, v = z[:, :W], z[:, W:2 * W], z[:, 2 * W:3 * W]
    lora = z[:, 3 * W:3 * W + SLOT]
    gd = z[:, 3 * W + SLOT:3 * W + SLOT + GATE_LORA]
    wl = w0_ref[...] + _dot(jnp.tanh(lora).astype(BF16), w2_ref[...])
    w_log = -jax.nn.softplus(-wl) - 0.5
    lw = -jnp.exp(w_log)
    a = jax.nn.sigmoid(a0_ref[...] + _dot(lora.astype(BF16), a2_ref[...]))
    g = _dot(jax.nn.sigmoid(gd).astype(BF16), g2_ref[...])
    kk = k * kk_ref[...]
    bd = bd_ref[...]
    kk = kk / jnp.maximum(jnp.sqrt(_dot3(kk * kk, bd)), 1e-12)
    k2 = k * (1.0 + (a - 1.0) * ka_ref[...])
    bon = _dot3(r * k2 * rk_ref[...], bd) * v
    outs = ((r_o, r), (k_o, k2), (v_o, v), (kk_o, kk), (b_o, kk * a), (lw_o, lw), (g_o, g), (bon_o, bon))
    for ref, val in outs:
        for h in range(B_HEADS):
            ref[0, h] = val[:, h * HEAD_DIM:(h + 1) * HEAD_DIM]


def _rw_pre(zb, mu, w0, w2e, a0, a2e, g2, k_k, k_a, r_k, bd):
    B, T, NB = zb.shape
    tm = 256
    W = B_WIDTH
    fix = lambda b, i: (0, 0)
    hm = pl.BlockSpec((1, B_HEADS, tm, HEAD_DIM), lambda b, i: (b, 0, i, 0))
    return pl.pallas_call(
        _rw_pre_kernel,
        grid=(B, T // tm),
        in_specs=[pl.BlockSpec((1, tm, NB), lambda b, i: (b, i, 0)),
                  pl.BlockSpec((1, NB), fix),
                  pl.BlockSpec((1, W), fix), pl.BlockSpec((SLOT, W), fix),
                  pl.BlockSpec((1, W), fix), pl.BlockSpec((SLOT, W), fix),
                  pl.BlockSpec((GATE_LORA, W), fix),
                  pl.BlockSpec((1, W), fix), pl.BlockSpec((1, W), fix), pl.BlockSpec((1, W), fix),
                  pl.BlockSpec((W, W), fix)],
        out_specs=[hm] * 8,
        out_shape=[jax.ShapeDtypeStruct((B, B_HEADS, T, HEAD_DIM), F32)] * 8,
        scratch_shapes=[pltpu.VMEM((8, NB), F32)],
        compiler_params=_cparams("parallel", "arbitrary"),
        name="rwkv_prep",
    )(zb, mu, w0, w2e, a0, a2e, g2, k_k, k_a, r_k, bd)


def _rw_chunk_kernel(r_ref, k_ref, v_ref, kk_ref, b_ref, lw_ref, ry_o, yc_o, pm_o, qm_o):
    r, k, v, kk, b, lw = (x[0] for x in (r_ref, k_ref, v_ref, kk_ref, b_ref, lw_ref))
    H, C, N = r.shape
    ti = lax.broadcasted_iota(jnp.int32, (H, C, C), 1)
    si = lax.broadcasted_iota(jnp.int32, (H, C, C), 2)
    tri = jnp.where(si <= ti, 1.0, 0.0).astype(BF16)
    h1, h2, h3 = _split3(lw)
    cum = _bmm(tri, h1) + _bmm(tri, h2) + _bmm(tri, h3)
    tot = cum[:, C - 1:C, :]
    e_in = jnp.exp(cum)
    e_out = jnp.exp(-cum)
    a_s = (kk * jnp.exp(cum - lw)).astype(BF16)
    r_s = r * e_in
    r_b = r_s.astype(BF16)
    b_s = (b * e_out).astype(BF16)
    k_s = (k * e_out).astype(BF16)
    e_end = jnp.exp(tot - cum)
    b_e = (b * e_end).astype(BF16)
    k_e = (k * e_end).astype(BF16)
    vb = v.astype(BF16)
    low = si < ti
    lowi = si <= ti
    l_ab = jnp.where(low, _bmm_nt(a_s, b_s), 0.0)
    l_ak = jnp.where(low, _bmm_nt(a_s, k_s), 0.0).astype(BF16)
    m_rb = jnp.where(lowi, _bmm_nt(r_b, b_s), 0.0).astype(BF16)
    m_rk = jnp.where(lowi, _bmm_nt(r_b, k_s), 0.0).astype(BF16)
    inv = jnp.where(si == ti, 1.0, 0.0)
    m = 1
    while m < C:
        pair = (((ti // m) % 2) == 1) & ((si // m) == (ti // m) - 1)
        lm = jnp.where(pair, l_ab, 0.0).astype(BF16)
        ib = inv.astype(BF16)
        inv = inv - _bmm(_bmm(ib, lm).astype(BF16), ib)
        m *= 2
    ib = inv.astype(BF16)
    w1 = _bmm(ib, a_s).astype(BF16)
    w2 = _bmm(ib, _bmm(l_ak, vb).astype(BF16)).astype(BF16)
    ry_o[0] = r_s - _bmm(m_rb, w1)
    yc_o[0] = _bmm(m_rk, vb) - _bmm(m_rb, w2)
    ji = lax.broadcasted_iota(jnp.int32, (H, N, N), 1)
    di = lax.broadcasted_iota(jnp.int32, (H, N, N), 2)
    decay = jnp.where(ji == di, jnp.broadcast_to(jnp.exp(tot), (H, N, N)), 0.0)
    pm_o[0] = decay - _bmm_tn(b_e, w1)
    qm_o[0] = _bmm_tn(k_e, vb) - _bmm_tn(b_e, w2)


def _rw_chunks(r, k, v, kk, b, lw):
    B, H, T, N = r.shape
    C = RW_CHUNK
    nc = T // C
    tok = pl.BlockSpec((1, H, C, N), lambda bi, c: (bi, 0, c, 0))
    return pl.pallas_call(
        _rw_chunk_kernel,
        grid=(B, nc),
        in_specs=[tok] * 6,
        out_specs=[tok, tok, tok, tok],
        out_shape=[jax.ShapeDtypeStruct((B, H, T, N), F32)] * 2
        + [jax.ShapeDtypeStruct((B, H, nc * N, N), F32)] * 2,
        compiler_params=_cparams("parallel", "parallel"),
        name="rwkv_chunk_ops",
    )(r, k, v, kk, b, lw)


def _rw_scan_kernel(ry_ref, yc_ref, pm_ref, qm_ref, g_ref, bon_ref, lnw_ref, lnb_ref, o_ref, state):
    @pl.when(pl.program_id(1) == 0)
    def _():
        state[...] = jnp.zeros_like(state)

    g0 = state[...]
    gh = g0.astype(BF16)
    gl = (g0 - gh.astype(F32)).astype(BF16)
    ry = ry_ref[0].astype(BF16)
    pm = pm_ref[0].astype(BF16)
    y = _bmm(ry, gh) + _bmm(ry, gl) + yc_ref[0]
    state[...] = _bmm(pm, gh) + _bmm(pm, gl) + qm_ref[0]
    mean = jnp.mean(y, axis=-1, keepdims=True)
    d = y - mean
    var = jnp.mean(d * d, axis=-1, keepdims=True)
    yn = d * lax.rsqrt(var + GN_EPS) * lnw_ref[...] + lnb_ref[...]
    o_ref[0] = (yn + bon_ref[0]) * g_ref[0]


def _rw_scan(ry, yc, pm, qm, g, bon, lnw, lnb):
    B, H, T, N = ry.shape
    C = RW_CHUNK
    tok = pl.BlockSpec((1, H, C, N), lambda bi, c: (bi, 0, c, 0))
    mat = pl.BlockSpec((1, H, N, N), lambda bi, c: (bi, 0, c, 0))
    par = pl.BlockSpec((H, 1, N), lambda bi, c: (0, 0, 0))
    return pl.pallas_call(
        _rw_scan_kernel,
        grid=(B, T // C),
        in_specs=[tok, tok, mat, mat, tok, tok, par, par],
        out_specs=tok,
        out_shape=jax.ShapeDtypeStruct((B, H, T, N), F32),
        scratch_shapes=[pltpu.VMEM((H, N, N), F32)],
        compiler_params=_cparams("parallel", "arbitrary"),
        name="rwkv_scan",
    )(ry, yc, pm, qm, g, bon, lnw, lnb)


def _out_kernel(x_ref, oa_ref, ob_ref, oc_ref, woa_ref, wob_ref, woc_ref, g1_ref, o_ref):
    acc = _dot(oa_ref[0].astype(BF16), woa_ref[...]) + _dot(oc_ref[0].astype(BF16), woc_ref[...])
    for h in range(B_HEADS):
        acc = acc + _dot(ob_ref[0, h].astype(BF16), wob_ref[h])
    o_ref[0] = x_ref[0] + g1_ref[0] * acc


def _out_proj(x, oa, ob, oc, woa, wob, woc, g1):
    B, T, D = x.shape
    tm = 512
    row = lambda b, i: (b, i, 0)
    fix2 = lambda b, i: (0, 0)
    return pl.pallas_call(
        _out_kernel,
        grid=(B, T // tm),
        in_specs=[pl.BlockSpec((1, tm, D), row),
                  pl.BlockSpec((1, tm, oa.shape[2]), row),
                  pl.BlockSpec((1, B_HEADS, tm, HEAD_DIM), lambda b, i: (b, 0, i, 0)),
                  pl.BlockSpec((1, tm, oc.shape[2]), row),
                  pl.BlockSpec(woa.shape, fix2),
                  pl.BlockSpec(wob.shape, lambda b, i: (0, 0, 0)),
                  pl.BlockSpec(woc.shape, fix2),
                  pl.BlockSpec((1, 1, D), lambda b, i: (b, 0, 0))],
        out_specs=pl.BlockSpec((1, tm, D), row),
        out_shape=jax.ShapeDtypeStruct((B, T, D), F32),
        compiler_params=_cparams("parallel", "parallel"),
        name="out_proj",
    )(x, oa, ob, oc, woa, wob, woc, g1.reshape(B, 1, D))


def _ffn_kernel(x_ref, g_ref, sc_ref, sh_ref, g2_ref, wg_ref, wu_ref, wo_ref, o_ref, h_s, acc_s):
    j = pl.program_id(2)

    @pl.when(j == 0)
    def _():
        x = x_ref[0]
        y = x * lax.rsqrt(jnp.mean(x * x, axis=-1, keepdims=True) + NORM_EPS) * g_ref[...]
        h_s[...] = (y * (1.0 + sc_ref[0]) + sh_ref[0]).astype(BF16)
        acc_s[...] = jnp.zeros_like(acc_s)

    h = h_s[...]
    gate = _dot(h, wg_ref[...])
    up = _dot(h, wu_ref[...])
    act = (gate * jax.nn.sigmoid(gate) * up).astype(BF16)
    acc_s[...] += _dot(act, wo_ref[...])

    @pl.when(j == pl.num_programs(2) - 1)
    def _():
        o_ref[0] = x_ref[0] + g2_ref[0] * acc_s[...]


def _ffn(x, g, sc, sh, g2, wi, wo):
    B, T, D = x.shape
    F = wo.shape[0]
    tm, th = 1024, 256
    nh = F // th
    row = lambda b, i, j: (b, i, 0)
    per_b = lambda b, i, j: (b, 0, 0)
    return pl.pallas_call(
        _ffn_kernel,
        grid=(B, T // tm, nh),
        in_specs=[pl.BlockSpec((1, tm, D), row),
                  pl.BlockSpec((1, D), lambda b, i, j: (0, 0)),
                  pl.BlockSpec((1, 1, D), per_b),
                  pl.BlockSpec((1, 1, D), per_b),
                  pl.BlockSpec((1, 1, D), per_b),
                  pl.BlockSpec((D, th), lambda b, i, j: (0, j)),
                  pl.BlockSpec((D, th), lambda b, i, j: (0, j + nh)),
                  pl.BlockSpec((th, D), lambda b, i, j: (j, 0))],
        out_specs=pl.BlockSpec((1, tm, D), row),
        out_shape=jax.ShapeDtypeStruct((B, T, D), F32),
        scratch_shapes=[pltpu.VMEM((tm, D), BF16), pltpu.VMEM((tm, D), F32)],
        compiler_params=_cparams("parallel", "parallel", "arbitrary"),
        name="ffn_swiglu",
    )(x, g.reshape(1, D), sc.reshape(B, 1, D), sh.reshape(B, 1, D), g2.reshape(B, 1, D), wi, wi, wo)


def _rot_cols(w):
    half = HEAD_DIM // 2
    return jnp.concatenate([-w[..., half:], w[..., :half]], axis=-1)


def _rope_slot(w):
    return jnp.concatenate([w, _rot_cols(w)], axis=-1)


def _pad_slot(w):
    return jnp.pad(w, [(0, 0)] * (w.ndim - 1) + [(0, SLOT - w.shape[-1])])


def _perm_gain(g):
    half = HEAD_DIM // 2
    return jnp.concatenate([g, g[half:], g[:half]]).reshape(1, SLOT)


def _rope_table(pos):
    half = HEAD_DIM // 2
    inv = ROPE_THETA ** (-np.arange(half, dtype=np.float64) / half)
    ang = np.asarray(pos, np.float64)[:, None] * inv[None, :]
    cos, sin = np.cos(ang), np.sin(ang)
    return jnp.asarray(np.concatenate([cos, cos, sin, sin], axis=1), F32)


def _in_weights(w):
    D = w.shape[0]
    hd = HEAD_DIM
    o = 0
    qa = w[:, o:o + A_HEADS * hd]; o += A_HEADS * hd
    ka = w[:, o:o + hd]; o += hd
    va = w[:, o:o + hd]; o += hd
    iq = w[:, o:o + IDX_HEADS * hd]; o += IDX_HEADS * hd
    ik = w[:, o:o + hd]; o += hd
    iw = w[:, o:o + IDX_HEADS]; o += IDX_HEADS
    nb = 3 * B_WIDTH + DECAY_LORA + AAA_LORA + GATE_LORA
    wb = w[:, o:o + nb]; o += nb
    qc = w[:, o:o + C_HEADS * hd]; o += C_HEADS * hd
    kc, vc, ksl, vsl, kwn, vwn = (w[:, o + i * hd:o + (i + 1) * hd] for i in range(6)); o += 6 * hd
    gc = w[:, o:o + 3 * C_HEADS]
    heads = lambda m, n: [m[:, i * hd:(i + 1) * hd] for i in range(n)]
    wa = jnp.concatenate([_rope_slot(h) for h in heads(qa, A_HEADS)]
                         + [_rope_slot(h) for h in heads(iq, IDX_HEADS)]
                         + [_rope_slot(ka), _rope_slot(ik), _pad_slot(va), _pad_slot(iw)], axis=1)
    wc = jnp.concatenate([_rope_slot(h) for h in heads(qc, C_HEADS)]
                         + [_rope_slot(ksl), _rope_slot(kwn), jnp.concatenate([kc, vc], axis=1),
                            _pad_slot(vsl), _pad_slot(vwn), _pad_slot(gc)], axis=1)
    return wa.astype(BF16), wb.astype(BF16), wc.astype(BF16)


def _cmp_weights(pe, w1, w2):
    half = CMP_LEN // 2
    hd = HEAD_DIM
    zero = jnp.zeros((half, hd, CMP_HID), F32)

    def expand(lo):
        wk = w1[0].reshape(CMP_LEN, hd, CMP_HID)[lo:lo + half]
        wv = w1[1].reshape(CMP_LEN, hd, CMP_HID)[lo:lo + half]
        k_rows = jnp.concatenate([wk, zero], axis=1)
        v_rows = jnp.concatenate([zero, wv], axis=1)
        return jnp.concatenate([k_rows, v_rows], axis=2).reshape(half * SLOT, 2 * CMP_HID).astype(BF16)

    def pe_row(lo):
        return jnp.concatenate([pe[0, lo:lo + half], pe[1, lo:lo + half]], axis=1).reshape(1, half * SLOT)

    w2k = _rope_slot(w2[0]).astype(BF16)
    w2v = _pad_slot(w2[1]).astype(BF16)
    return pe_row(0), pe_row(half), expand(0), expand(half), w2k, w2v


def _nsa_tables(T):
    n_cmp = (T - CMP_LEN) // CMP_STRIDE + 1
    n_blk = T // SEL_LEN
    starts = np.arange(n_cmp) * CMP_STRIDE
    end_pos = starts + CMP_LEN - 1
    sel_start = np.arange(n_blk) * SEL_LEN
    ovl = np.zeros((SLOT, SLOT), np.float32)
    ovl[:n_cmp, :n_blk] = ((starts[:, None] <= sel_start[None, :] + SEL_LEN - 1)
                           & (end_pos[:, None] >= sel_start[None, :]))
    expand = np.zeros((SLOT, T), np.float32)
    expand[np.arange(T) // SEL_LEN, np.arange(T)] = 1.0
    m = T // CMP_STRIDE
    cse = _rope_table(np.arange(m) * CMP_STRIDE + CMP_LEN - 1)
    return n_cmp, jnp.asarray(ovl, BF16), jnp.asarray(expand, BF16), cse


def _out_weights(w):
    hd = HEAD_DIM
    D = w.shape[1]

    def slots(base, n):
        blocks = [jnp.concatenate([w[base + i * hd:base + (i + 1) * hd], jnp.zeros((SLOT - hd, D), F32)])
                  for i in range(n)]
        return jnp.concatenate(blocks).astype(BF16)

    a0 = 0
    b0 = A_HEADS * hd
    c0 = b0 + B_WIDTH
    return slots(a0, A_HEADS), w[b0:c0].reshape(B_HEADS, hd, D).astype(BF16), slots(c0, C_HEADS)


def kernel(x, c, ada_w, ada_b, norm1_g, w_in, dsa_q_g, dsa_k_g, rwkv_mu, rwkv_w0, rwkv_w2, rwkv_a0, rwkv_a2, rwkv_g2, rwkv_k_k, rwkv_k_a, rwkv_r_k, rwkv_ln_w, rwkv_ln_b, nsa_q_g, nsa_k_g, nsa_pe, nsa_w1, nsa_w2, w_out, norm2_g, ffn_wi, ffn_wo):
    B, T, D = x.shape
    L = w_in.shape[0]
    W = B_WIDTH
    assert T % 256 == 0 and T % RW_CHUNK == 0 and D % SLOT == 0
    mod = _modulation(c, ada_w, ada_b)
    cs = _rope_table(np.arange(T))
    n_cmp, ovl, expand, cse = _nsa_tables(T)
    bd = jnp.asarray(np.kron(np.eye(B_HEADS), np.ones((HEAD_DIM, HEAD_DIM))), BF16)
    lora_pad = jnp.zeros((SLOT - DECAY_LORA, W), F32)
    for l in range(L):
        sh1, sc1, g1, sh2, sc2, g2 = (mod[l, :, i * D:(i + 1) * D] for i in range(6))
        wa, wb, wc = _in_weights(w_in[l])
        za, zb, zc = _in_proj(x, norm1_g[l], sc1, sh1, wa, wb, wc)
        qa, iq, ka, ik, va, iw = _dsa_prep(za, cs, _perm_gain(dsa_q_g[l]), _perm_gain(dsa_k_g[l]))
        oa = _dsa(qa, iq, iw, ka, ik, va)
        w2e = jnp.concatenate([rwkv_w2[l], lora_pad]).astype(BF16)
        a2e = jnp.concatenate([lora_pad, rwkv_a2[l]]).astype(BF16)
        row = lambda p: p.reshape(1, W)
        r, k2, v, kk, bb, lw, gg, bon = _rw_pre(
            zb, rwkv_mu[l].reshape(1, -1), row(rwkv_w0[l]), w2e, row(rwkv_a0[l]), a2e,
            rwkv_g2[l].astype(BF16), row(rwkv_k_k[l]), row(rwkv_k_a[l]), row(rwkv_r_k[l]), bd)
        ry, yc, pm, qm = _rw_chunks(r, k2, v, kk, bb, lw)
        ob = _rw_scan(ry, yc, pm, qm, gg, bon, rwkv_ln_w[l].reshape(B_HEADS, 1, HEAD_DIM),
                      rwkv_ln_b[l].reshape(B_HEADS, 1, HEAD_DIM))
        gkc = _perm_gain(nsa_k_g[l])
        qc, ks, kw, vs, vw, gc = _nsa_prep(zc, cs, _perm_gain(nsa_q_g[l]), gkc)
        kv_slot = (C_HEADS + 2) * SLOT
        rm = zc[:, :, kv_slot:kv_slot + SLOT].reshape(B, T // CMP_STRIDE, CMP_STRIDE * SLOT)
        kc, vc = _nsa_compress(rm, *_cmp_weights(nsa_pe[l], nsa_w1[l], nsa_w2[l]), cse, gkc)
        oc = _nsa(qc, gc, kc, vc, ks, vs, kw, vw, ovl, expand, n_cmp)
        woa, wob, woc = _out_weights(w_out[l])
        x = _out_proj(x, oa, ob, oc, woa, wob, woc, g1)
        x = _ffn(x, norm2_g[l], sc2, sh2, g2, ffn_wi[l].astype(BF16), ffn_wo[l].astype(BF16))
    return x
```

```python
import functools

import numpy as np
import jax
import jax.numpy as jnp
from jax import lax
from jax.experimental import pallas as pl
from jax.experimental.pallas import tpu as pltpu

F32 = jnp.float32
BF16 = jnp.bfloat16

HEAD_DIM = 64
SLOT = 128
A_HEADS = 4
IDX_HEADS = 4
DSA_TOPK = 256
B_HEADS = 8
B_WIDTH = B_HEADS * HEAD_DIM
DECAY_LORA = 64
AAA_LORA = 64
GATE_LORA = 128
GN_EPS = 64e-5
C_HEADS = 4
CMP_LEN = 32
CMP_STRIDE = 16
CMP_HID = 256
SEL_LEN = 64
SEL_TOPN = 16
WIN = 512
Q_BLOCK = 128
ROPE_THETA = 10000.0
NORM_EPS = 1e-6
NEG = -1e30
INT_MIN = -2147483648
RW_CHUNK = 64
RW_SCAN_CHUNKS = 4
PREFIX_GROUPS = 8
VMEM_LIMIT_BYTES = 56 * 1024 * 1024


def _cparams(*sem):
    return pltpu.CompilerParams(dimension_semantics=sem, vmem_limit_bytes=VMEM_LIMIT_BYTES)


def _dot(a, b):
    return jnp.dot(a, b, preferred_element_type=F32)


def _dot_nt(a, b):
    return lax.dot_general(a, b, (((1,), (1,)), ((), ())), preferred_element_type=F32)


def _split3(x):
    hi = x.astype(BF16)
    r1 = x - hi.astype(F32)
    mid = r1.astype(BF16)
    lo = (r1 - mid.astype(F32)).astype(BF16)
    return hi, mid, lo


def _dot3(x, w):
    hi, mid, lo = _split3(x)
    return _dot(hi, w) + _dot(mid, w) + _dot(lo, w)


def _bmm(a, b):
    return jnp.einsum('hts,hsd->htd', a, b, preferred_element_type=F32)


def _bmm_nt(a, b):
    return jnp.einsum('htj,hsj->hts', a, b, preferred_element_type=F32)


def _bmm_tn(a, b):
    return jnp.einsum('htj,htd->hjd', a, b, preferred_element_type=F32)


def _mod_kernel(c_ref, w_ref, b_ref, o_ref):
    c = c_ref[...]
    s = (c * jax.nn.sigmoid(c)).astype(BF16)
    o_ref[0] = _dot(s, w_ref[0]) + b_ref[0]


def _modulation(c, ada_w, ada_b):
    L, D, N = ada_w.shape
    B = c.shape[0]
    tn = 1536
    return pl.pallas_call(
        _mod_kernel,
        grid=(L, N // tn),
        in_specs=[pl.BlockSpec((B, D), lambda l, j: (0, 0)),
                  pl.BlockSpec((1, D, tn), lambda l, j: (l, 0, j)),
                  pl.BlockSpec((1, 1, tn), lambda l, j: (l, 0, j))],
        out_specs=pl.BlockSpec((1, B, tn), lambda l, j: (l, 0, j)),
        out_shape=jax.ShapeDtypeStruct((L, B, N), F32),
        compiler_params=_cparams("parallel", "parallel"),
        name="adaln_modulation",
    )(c, ada_w.astype(BF16), ada_b.reshape(L, 1, N))


def _in_kernel(x_ref, g_ref, sc_ref, sh_ref, wa_ref, wb_ref, wc_ref, za_ref, zb_ref, zc_ref):
    x = x_ref[0]
    y = x * lax.rsqrt(jnp.mean(x * x, axis=-1, keepdims=True) + NORM_EPS) * g_ref[...]
    h = (y * (1.0 + sc_ref[0]) + sh_ref[0]).astype(BF16)
    za_ref[0] = _dot(h, wa_ref[...])
    zb_ref[0] = _dot(h, wb_ref[...])
    zc_ref[0] = _dot(h, wc_ref[...])


def _in_proj(x, g, sc, sh, wa, wb, wc):
    B, T, D = x.shape
    tm = 256
    na, nb, nc = wa.shape[1], wb.shape[1], wc.shape[1]
    row = lambda b, i: (b, i, 0)
    fix = lambda b, i: (0, 0)
    per_b = lambda b, i: (b, 0, 0)
    return pl.pallas_call(
        _in_kernel,
        grid=(B, T // tm),
        in_specs=[pl.BlockSpec((1, tm, D), row),
                  pl.BlockSpec((1, D), fix),
                  pl.BlockSpec((1, 1, D), per_b),
                  pl.BlockSpec((1, 1, D), per_b),
                  pl.BlockSpec((D, na), fix),
                  pl.BlockSpec((D, nb), fix),
                  pl.BlockSpec((D, nc), fix)],
        out_specs=[pl.BlockSpec((1, tm, na), row),
                   pl.BlockSpec((1, tm, nb), row),
                   pl.BlockSpec((1, tm, nc), row)],
        out_shape=[jax.ShapeDtypeStruct((B, T, na), F32),
                   jax.ShapeDtypeStruct((B, T, nb), F32),
                   jax.ShapeDtypeStruct((B, T, nc), F32)],
        compiler_params=_cparams("parallel", "parallel"),
        name="in_proj",
    )(x, g.reshape(1, D), sc.reshape(B, 1, D), sh.reshape(B, 1, D), wa, wb, wc)


def _norm_rope(x, table):
    rs = lax.rsqrt(jnp.mean(x * x, axis=-1, keepdims=True) + NORM_EPS)
    return x * rs * table


def _fold(y):
    return y + pltpu.roll(y, HEAD_DIM, 1)


def _dsa_prep_kernel(za_ref, cs_ref, gq_ref, gk_ref, qa_ref, iq_ref, ka_ref, ik_ref, va_ref, iw_ref):
    cs = cs_ref[...]
    tq = cs * gq_ref[...] * (HEAD_DIM ** -0.5)
    tk = cs * gk_ref[...]
    for h in range(A_HEADS):
        sl = slice(h * SLOT, (h + 1) * SLOT)
        qa_ref[0, :, sl] = _norm_rope(za_ref[0, :, sl], tq).astype(BF16)
    for h in range(IDX_HEADS):
        sl = slice(h * SLOT, (h + 1) * SLOT)
        src = slice((A_HEADS + h) * SLOT, (A_HEADS + h + 1) * SLOT)
        iq_ref[0, :, sl] = (za_ref[0, :, src] * cs).astype(BF16)
    base = (A_HEADS + IDX_HEADS) * SLOT
    ka_ref[0] = _fold(_norm_rope(za_ref[0, :, base:base + SLOT], tk)).astype(BF16)
    ik_ref[0] = _fold(za_ref[0, :, base + SLOT:base + 2 * SLOT] * cs).astype(BF16)
    va_ref[0] = za_ref[0, :, base + 2 * SLOT:base + 3 * SLOT].astype(BF16)
    iw_ref[0] = za_ref[0, :, base + 3 * SLOT:base + 4 * SLOT]


def _dsa_prep(za, cs, gq, gk):
    B, T, NA = za.shape
    tm = 256
    row = lambda b, i: (b, i, 0)
    fix = lambda b, i: (0, 0)
    outs = [(A_HEADS * SLOT, BF16), (IDX_HEADS * SLOT, BF16), (SLOT, BF16), (SLOT, BF16), (SLOT, BF16),
            (SLOT, F32)]
    return pl.pallas_call(
        _dsa_prep_kernel,
        grid=(B, T // tm),
        in_specs=[pl.BlockSpec((1, tm, NA), row),
                  pl.BlockSpec((tm, SLOT), lambda b, i: (i, 0)),
                  pl.BlockSpec((1, SLOT), fix),
                  pl.BlockSpec((1, SLOT), fix)],
        out_specs=[pl.BlockSpec((1, tm, w), row) for w, _ in outs],
        out_shape=[jax.ShapeDtypeStruct((B, T, w), dt) for w, dt in outs],
        compiler_params=_cparams("parallel", "parallel"),
        name="dsa_prep",
    )(za, cs, gq, gk)


def _masked_attend(q_ref, k, v, mask, heads):
    for h in range(heads):
        sl = slice(h * SLOT, (h + 1) * SLOT)
        s = jnp.where(mask, _dot_nt(q_ref[0, :, sl], k), NEG)
        m = jnp.max(s, axis=1, keepdims=True)
        p = jnp.exp(s - m)
        l = jnp.sum(p, axis=1, keepdims=True)
        yield h, _dot(p.astype(BF16), v) / l


def _dsa_block(qa_ref, iq_ref, iw_ref, ka_ref, ik_ref, va_ref, o_ref, t0, Te, topk):
    rows = t0 + lax.broadcasted_iota(jnp.int32, (Q_BLOCK, 1), 0)
    cols = lax.broadcasted_iota(jnp.int32, (1, Te), 1)
    causal = cols <= rows
    if Te <= topk:
        mask = causal
    else:
        ik = ik_ref[0, :Te, :]
        iw = iw_ref[0]
        score = jnp.zeros((Q_BLOCK, Te), F32)
        for h in range(IDX_HEADS):
            s = _dot_nt(iq_ref[0, :, h * SLOT:(h + 1) * SLOT], ik)
            score = score + iw[:, h:h + 1] * jnp.maximum(s, 0.0)
        score = jnp.where(score == 0.0, 0.0, score)
        bits = lax.bitcast_convert_type(score, jnp.int32)
        key = bits ^ ((bits >> 31) & jnp.int32(0x7FFFFFFF))
        key = jnp.where(causal, key, jnp.int32(INT_MIN))
        kf = float(topk)

        def count_ge(cand):
            return jnp.sum(jnp.where(key >= cand, 1.0, 0.0), axis=1, keepdims=True)

        thr0 = jnp.where(count_ge(jnp.int32(0)) >= kf, jnp.int32(0), jnp.int32(INT_MIN))

        def body(i, thr):
            cand = thr | jnp.left_shift(jnp.int32(1), 30 - i)
            return jnp.where(count_ge(cand) >= kf, cand, thr)

        thr = lax.fori_loop(0, 31, body, thr0)
        gt = key > thr
        eq = key == thr
        need = kf - jnp.sum(jnp.where(gt, 1.0, 0.0), axis=1, keepdims=True)
        ri = lax.broadcasted_iota(jnp.int32, (SLOT, SLOT), 0)
        ci = lax.broadcasted_iota(jnp.int32, (SLOT, SLOT), 1)
        upper = jnp.where(ri <= ci, 1.0, 0.0).astype(BF16)
        run = jnp.zeros((Q_BLOCK, 1), F32)
        sel = []
        for c in range(Te // SLOT):
            sl = slice(c * SLOT, (c + 1) * SLOT)
            e = jnp.where(eq[:, sl], 1.0, 0.0)
            incl = _dot(e.astype(BF16), upper)
            sel.append(gt[:, sl] | (eq[:, sl] & (incl - e + run < need)))
            run = run + incl[:, SLOT - 1:SLOT]
        mask = jnp.concatenate(sel, axis=1) & causal
    for h, o in _masked_attend(qa_ref, ka_ref[0, :Te, :], va_ref[0, :Te, :], mask, A_HEADS):
        o_ref[0, :, h * SLOT:(h + 1) * SLOT] = o


def _dsa_kernel(qa_ref, iq_ref, iw_ref, ka_ref, ik_ref, va_ref, o_ref, *, T, topk, groups):
    qi = pl.program_id(1)
    per = (T // Q_BLOCK) // groups
    for g in range(groups):
        @pl.when(qi // per == g)
        def _(g=g):
            _dsa_block(qa_ref, iq_ref, iw_ref, ka_ref, ik_ref, va_ref, o_ref, qi * Q_BLOCK,
                       (g + 1) * per * Q_BLOCK, topk)


def _dsa(qa, iq, iw, ka, ik, va):
    B, T, _ = qa.shape
    topk = min(DSA_TOPK, T // 4)
    blk = lambda b, i: (b, i, 0)
    full = lambda b, i: (b, 0, 0)
    return pl.pallas_call(
        functools.partial(_dsa_kernel, T=T, topk=topk, groups=PREFIX_GROUPS),
        grid=(B, T // Q_BLOCK),
        in_specs=[pl.BlockSpec((1, Q_BLOCK, A_HEADS * SLOT), blk),
                  pl.BlockSpec((1, Q_BLOCK, IDX_HEADS * SLOT), blk),
                  pl.BlockSpec((1, Q_BLOCK, SLOT), blk),
                  pl.BlockSpec((1, T, SLOT), full),
                  pl.BlockSpec((1, T, SLOT), full),
                  pl.BlockSpec((1, T, SLOT), full)],
        out_specs=pl.BlockSpec((1, Q_BLOCK, A_HEADS * SLOT), blk),
        out_shape=jax.ShapeDtypeStruct((B, T, A_HEADS * SLOT), F32),
        compiler_params=_cparams("parallel", "parallel"),
        name="dsa_attention",
    )(qa, iq, iw, ka, ik, va)


def _nsa_prep_kernel(zc_ref, cs_ref, gq_ref, gk_ref, qc_ref, ks_ref, kw_ref, vs_ref, vw_ref, gc_ref):
    cs = cs_ref[...]
    tq = cs * gq_ref[...] * (HEAD_DIM ** -0.5)
    tk = cs * gk_ref[...]
    for h in range(C_HEADS):
        sl = slice(h * SLOT, (h + 1) * SLOT)
        qc_ref[0, :, sl] = _norm_rope(zc_ref[0, :, sl], tq).astype(BF16)
    base = C_HEADS * SLOT
    ks_ref[0] = _fold(_norm_rope(zc_ref[0, :, base:base + SLOT], tk)).astype(BF16)
    kw_ref[0] = _fold(_norm_rope(zc_ref[0, :, base + SLOT:base + 2 * SLOT], tk)).astype(BF16)
    vs_ref[0] = zc_ref[0, :, base + 3 * SLOT:base + 4 * SLOT].astype(BF16)
    vw_ref[0] = zc_ref[0, :, base + 4 * SLOT:base + 5 * SLOT].astype(BF16)
    gc_ref[0] = zc_ref[0, :, base + 5 * SLOT:base + 6 * SLOT]


def _nsa_prep(zc, cs, gq, gk):
    B, T, NC = zc.shape
    tm = 256
    row = lambda b, i: (b, i, 0)
    fix = lambda b, i: (0, 0)
    outs = [(C_HEADS * SLOT, BF16), (SLOT, BF16), (SLOT, BF16), (SLOT, BF16), (SLOT, BF16), (SLOT, F32)]
    return pl.pallas_call(
        _nsa_prep_kernel,
        grid=(B, T // tm),
        in_specs=[pl.BlockSpec((1, tm, NC), row),
                  pl.BlockSpec((tm, SLOT), lambda b, i: (i, 0)),
                  pl.BlockSpec((1, SLOT), fix),
                  pl.BlockSpec((1, SLOT), fix)],
        out_specs=[pl.BlockSpec((1, tm, w), row) for w, _ in outs],
        out_shape=[jax.ShapeDtypeStruct((B, T, w), dt) for w, dt in outs],
        compiler_params=_cparams("parallel", "parallel"),
        name="nsa_prep",
    )(zc, cs, gq, gk)


def _nsa_cmp_kernel(rm_ref, pet_ref, peb_ref, w1t_ref, w1b_ref, w2k_ref, w2v_ref, cse_ref, gk_ref,
                    kc_ref, vc_ref):
    rm = rm_ref[0]
    p = _dot((rm + pet_ref[...]).astype(BF16), w1t_ref[...])
    q = _dot((rm + peb_ref[...]).astype(BF16), w1b_ref[...])
    nrow = rm.shape[0]
    hid = jax.nn.gelu(p + pltpu.roll(q, nrow - 1, 0), approximate=True)
    kc = _dot(hid[:, :CMP_HID].astype(BF16), w2k_ref[...])
    kc_ref[0] = _fold(_norm_rope(kc, cse_ref[...] * gk_ref[...])).astype(BF16)
    vc_ref[0] = _dot(hid[:, CMP_HID:].astype(BF16), w2v_ref[...]).astype(BF16)


def _nsa_compress(rm, pet, peb, w1t, w1b, w2k, w2v, cse, gk):
    B, M, W = rm.shape
    fix = lambda b: (0, 0)
    return pl.pallas_call(
        _nsa_cmp_kernel,
        grid=(B,),
        in_specs=[pl.BlockSpec((1, M, W), lambda b: (b, 0, 0)),
                  pl.BlockSpec((1, W), fix), pl.BlockSpec((1, W), fix),
                  pl.BlockSpec(w1t.shape, fix), pl.BlockSpec(w1b.shape, fix),
                  pl.BlockSpec(w2k.shape, fix), pl.BlockSpec(w2v.shape, fix),
                  pl.BlockSpec(cse.shape, fix), pl.BlockSpec((1, SLOT), fix)],
        out_specs=[pl.BlockSpec((1, M, SLOT), lambda b: (b, 0, 0)),
                   pl.BlockSpec((1, M, SLOT), lambda b: (b, 0, 0))],
        out_shape=[jax.ShapeDtypeStruct((B, M, SLOT), BF16), jax.ShapeDtypeStruct((B, M, SLOT), BF16)],
        compiler_params=_cparams("parallel"),
        name="nsa_compress",
    )(rm, pet, peb, w1t, w1b, w2k, w2v, cse, gk)


def _nsa_kernel(qc_ref, gc_ref, kc_ref, vc_ref, ks_ref, vs_ref, kw_ref, vw_ref, ovl_ref, exp_ref, o_ref,
                slc_s, *, T, n_cmp, n_blk, top_n, win_len, groups):
    t0 = pl.program_id(1) * Q_BLOCK
    rows = t0 + lax.broadcasted_iota(jnp.int32, (Q_BLOCK, 1), 0)
    lane = lax.broadcasted_iota(jnp.int32, (1, SLOT), 1)
    is_cmp = lane < n_cmp
    cvalid = is_cmp & (lane * CMP_STRIDE + (CMP_LEN - 1) <= rows)
    kc = kc_ref[0]
    vc = vc_ref[0]
    psum = jnp.zeros((Q_BLOCK, SLOT), F32)
    o_cmp = []
    for h in range(C_HEADS):
        s = _dot_nt(qc_ref[0, :, h * SLOT:(h + 1) * SLOT], kc)
        s = jnp.where(is_cmp, jnp.where(cvalid, s, NEG), -jnp.inf)
        e = jnp.exp(s - jnp.max(s, axis=1, keepdims=True))
        p = jnp.where(cvalid, e / jnp.sum(e, axis=1, keepdims=True), 0.0)
        psum = psum + p
        o_cmp.append(_dot(p.astype(BF16), vc))
    imp = _dot3(psum, ovl_ref[...])
    cur = rows >> 6
    adm = lane * SEL_LEN <= rows
    forced = (lane == 0) | (lane == cur) | (lane == cur - 1)
    val = jnp.where(adm, jnp.where(forced, jnp.inf, imp), -jnp.inf)
    val = jnp.where(lane < n_blk, val, -jnp.inf)
    rank = jnp.zeros((Q_BLOCK, SLOT), F32)
    for j in range(n_blk):
        c = val[:, j:j + 1]
        rank = rank + jnp.where((c > val) | ((c == val) & (lane > j)), 1.0, 0.0)
    sel = jnp.where((rank < float(top_n)) & (lane < n_blk), 1.0, 0.0).astype(BF16)
    qi = pl.program_id(1)
    per = (T // Q_BLOCK) // groups
    for gi in range(groups):
        @pl.when(qi // per == gi)
        def _(gi=gi):
            te = (gi + 1) * per * Q_BLOCK
            cols = lax.broadcasted_iota(jnp.int32, (1, te), 1)
            mask = (_dot(sel, exp_ref[:, :te]) > 0.5) & (cols <= rows)
            for h, o in _masked_attend(qc_ref, ks_ref[0, :te, :], vs_ref[0, :te, :], mask, C_HEADS):
                slc_s[:, h * SLOT:(h + 1) * SLOT] = o
    ws = pl.multiple_of(jnp.clip(t0 - WIN, 0, T - win_len), Q_BLOCK)
    wpos = ws + lax.broadcasted_iota(jnp.int32, (1, win_len), 1)
    wmask = (wpos <= rows) & (wpos > rows - WIN)
    kw = kw_ref[0, pl.ds(ws, win_len), :]
    vw = vw_ref[0, pl.ds(ws, win_len), :]
    o_win = [o for _, o in _masked_attend(qc_ref, kw, vw, wmask, C_HEADS)]
    g = jax.nn.sigmoid(gc_ref[0])
    for h in range(C_HEADS):
        o_ref[0, :, h * SLOT:(h + 1) * SLOT] = (g[:, 3 * h:3 * h + 1] * o_cmp[h]
                                                + g[:, 3 * h + 1:3 * h + 2] * slc_s[:, h * SLOT:(h + 1) * SLOT]
                                                + g[:, 3 * h + 2:3 * h + 3] * o_win[h])


def _nsa(qc, gc, kc, vc, ks, vs, kw, vw, ovl, expand, n_cmp):
    B, T, _ = qc.shape
    n_blk = T // SEL_LEN
    blk = lambda b, i: (b, i, 0)
    full = lambda b, i: (b, 0, 0)
    fix = lambda b, i: (0, 0)
    M = kc.shape[1]
    kern = functools.partial(_nsa_kernel, T=T, n_cmp=n_cmp, n_blk=n_blk, top_n=min(SEL_TOPN, n_blk),
                             win_len=min(WIN + Q_BLOCK, T), groups=PREFIX_GROUPS)
    return pl.pallas_call(
        kern,
        grid=(B, T // Q_BLOCK),
        in_specs=[pl.BlockSpec((1, Q_BLOCK, C_HEADS * SLOT), blk),
                  pl.BlockSpec((1, Q_BLOCK, SLOT), blk),
                  pl.BlockSpec((1, M, SLOT), full),
                  pl.BlockSpec((1, M, SLOT), full),
                  pl.BlockSpec((1, T, SLOT), full),
                  pl.BlockSpec((1, T, SLOT), full),
                  pl.BlockSpec((1, T, SLOT), full),
                  pl.BlockSpec((1, T, SLOT), full),
                  pl.BlockSpec(ovl.shape, fix),
                  pl.BlockSpec(expand.shape, fix)],
        out_specs=pl.BlockSpec((1, Q_BLOCK, C_HEADS * SLOT), blk),
        out_shape=jax.ShapeDtypeStruct((B, T, C_HEADS * SLOT), F32),
        scratch_shapes=[pltpu.VMEM((Q_BLOCK, C_HEADS * SLOT), F32)],
        compiler_params=_cparams("parallel", "parallel"),
        name="nsa_attention",
    )(qc, gc, kc, vc, ks, vs, kw, vw, ovl, expand)


def _rw_pre_kernel(zb_ref, mu_ref, w0_ref, w2_ref, a0_ref, a2_ref, g2_ref, kk_ref, ka_ref, rk_ref, bd_ref,
                   r_o, k_o, v_o, kk_o, b_o, lw_o, g_o, bon_o, carry):
    @pl.when(pl.program_id(1) == 0)
    def _():
        carry[...] = jnp.zeros_like(carry)

    zt = zb_ref[0]
    tm = zt.shape[0]
    row0 = lax.broadcasted_iota(jnp.int32, (tm, 1), 0) == 0
    prev = jnp.where(row0, carry[7:8, :], pltpu.roll(zt, 1, 0))
    carry[...] = zt[tm - 8:tm, :]
    z = zt + (prev - zt) * mu_ref[...]
    W = B_WIDTH
    r, k, v = z[:, :W], z[:, W:2 * W], z[:, 2 * W:3 * W]
    lora = z[:, 3 * W:3 * W + SLOT]
    gd = z[:, 3 * W + SLOT:3 * W + SLOT + GATE_LORA]
    wl = w0_ref[...] + _dot(jnp.tanh(lora).astype(BF16), w2_ref[...])
    w_log = -jax.nn.softplus(-wl) - 0.5
    lw = -jnp.exp(w_log)
    a = jax.nn.sigmoid(a0_ref[...] + _dot(lora.astype(BF16), a2_ref[...]))
    g = _dot(jax.nn.sigmoid(gd).astype(BF16), g2_ref[...])
    kk = k * kk_ref[...]
    bd = bd_ref[...]
    kk = kk / jnp.maximum(jnp.sqrt(_dot3(kk * kk, bd)), 1e-12)
    k2 = k * (1.0 + (a - 1.0) * ka_ref[...])
    bon = _dot3(r * k2 * rk_ref[...], bd) * v
    outs = ((r_o, r), (k_o, k2), (v_o, v), (kk_o, kk), (b_o, kk * a), (lw_o, lw), (g_o, g), (bon_o, bon))
    for ref, val in outs:
        for h in range(B_HEADS):
            ref[0, h] = val[:, h * HEAD_DIM:(h + 1) * HEAD_DIM]


def _rw_pre(zb, mu, w0, w2e, a0, a2e, g2, k_k, k_a, r_k, bd):
    B, T, NB = zb.shape
    tm = 256
    W = B_WIDTH
    fix = lambda b, i: (0, 0)
    hm = pl.BlockSpec((1, B_HEADS, tm, HEAD_DIM), lambda b, i: (b, 0, i, 0))
    return pl.pallas_call(
        _rw_pre_kernel,
        grid=(B, T // tm),
        in_specs=[pl.BlockSpec((1, tm, NB), lambda b, i: (b, i, 0)),
                  pl.BlockSpec((1, NB), fix),
                  pl.BlockSpec((1, W), fix), pl.BlockSpec((SLOT, W), fix),
                  pl.BlockSpec((1, W), fix), pl.BlockSpec((SLOT, W), fix),
                  pl.BlockSpec((GATE_LORA, W), fix),
                  pl.BlockSpec((1, W), fix), pl.BlockSpec((1, W), fix), pl.BlockSpec((1, W), fix),
                  pl.BlockSpec((W, W), fix)],
        out_specs=[hm] * 8,
        out_shape=[jax.ShapeDtypeStruct((B, B_HEADS, T, HEAD_DIM), F32)] * 8,
        scratch_shapes=[pltpu.VMEM((8, NB), F32)],
        compiler_params=_cparams("parallel", "arbitrary"),
        name="rwkv_prep",
    )(zb, mu, w0, w2e, a0, a2e, g2, k_k, k_a, r_k, bd)


def _rw_chunk_kernel(r_ref, k_ref, v_ref, kk_ref, b_ref, lw_ref, ry_o, yc_o, pm_o, qm_o):
    r, k, v, kk, b, lw = (x[0] for x in (r_ref, k_ref, v_ref, kk_ref, b_ref, lw_ref))
    H, C, N = r.shape
    ti = lax.broadcasted_iota(jnp.int32, (H, C, C), 1)
    si = lax.broadcasted_iota(jnp.int32, (H, C, C), 2)
    tri = jnp.where(si <= ti, 1.0, 0.0).astype(BF16)
    h1, h2, h3 = _split3(lw)
    cum = _bmm(tri, h1) + _bmm(tri, h2) + _bmm(tri, h3)
    tot = cum[:, C - 1:C, :]
    e_in = jnp.exp(cum)
    e_out = jnp.exp(-cum)
    a_s = (kk * jnp.exp(cum - lw)).astype(BF16)
    r_s = r * e_in
    r_b = r_s.astype(BF16)
    b_s = (b * e_out).astype(BF16)
    k_s = (k * e_out).astype(BF16)
    e_end = jnp.exp(tot - cum)
    b_e = (b * e_end).astype(BF16)
    k_e = (k * e_end).astype(BF16)
    vb = v.astype(BF16)
    low = si < ti
    lowi = si <= ti
    l_ab = jnp.where(low, _bmm_nt(a_s, b_s), 0.0)
    l_ak = jnp.where(low, _bmm_nt(a_s, k_s), 0.0).astype(BF16)
    m_rb = jnp.where(lowi, _bmm_nt(r_b, b_s), 0.0).astype(BF16)
    m_rk = jnp.where(lowi, _bmm_nt(r_b, k_s), 0.0).astype(BF16)
    inv = jnp.where(si == ti, 1.0, 0.0)
    m = 1
    while m < C:
        pair = (((ti // m) % 2) == 1) & ((si // m) == (ti // m) - 1)
        lm = jnp.where(pair, l_ab, 0.0).astype(BF16)
        ib = inv.astype(BF16)
        inv = inv - _bmm(_bmm(ib, lm).astype(BF16), ib)
        m *= 2
    ib = inv.astype(BF16)
    w1 = _bmm(ib, a_s).astype(BF16)
    w2 = _bmm(ib, _bmm(l_ak, vb).astype(BF16)).astype(BF16)
    ry_o[0] = r_s - _bmm(m_rb, w1)
    yc_o[0] = _bmm(m_rk, vb) - _bmm(m_rb, w2)
    ji = lax.broadcasted_iota(jnp.int32, (H, N, N), 1)
    di = lax.broadcasted_iota(jnp.int32, (H, N, N), 2)
    decay = jnp.where(ji == di, jnp.broadcast_to(jnp.exp(tot), (H, N, N)), 0.0)
    pm_o[0] = decay - _bmm_tn(b_e, w1)
    qm_o[0] = _bmm_tn(k_e, vb) - _bmm_tn(b_e, w2)


def _rw_chunks(r, k, v, kk, b, lw):
    B, H, T, N = r.shape
    C = RW_CHUNK
    nc = T // C
    tok = pl.BlockSpec((1, H, C, N), lambda bi, c: (bi, 0, c, 0))
    return pl.pallas_call(
        _rw_chunk_kernel,
        grid=(B, nc),
        in_specs=[tok] * 6,
        out_specs=[tok, tok, tok, tok],
        out_shape=[jax.ShapeDtypeStruct((B, H, T, N), F32)] * 2
        + [jax.ShapeDtypeStruct((B, H, nc * N, N), F32)] * 2,
        compiler_params=_cparams("parallel", "parallel"),
        name="rwkv_chunk_ops",
    )(r, k, v, kk, b, lw)


def _rw_scan_kernel(ry_ref, yc_ref, pm_ref, qm_ref, g_ref, bon_ref, lnw_ref, lnb_ref, o_ref, state):
    @pl.when(pl.program_id(1) == 0)
    def _():
        state[...] = jnp.zeros_like(state)

    C = N = RW_CHUNK
    g0 = state[...]
    for i in range(ry_ref.shape[2] // C):
        tok = slice(i * C, (i + 1) * C)
        mat = slice(i * N, (i + 1) * N)
        gh = g0.astype(BF16)
        gl = (g0 - gh.astype(F32)).astype(BF16)
        ry = ry_ref[0, :, tok, :].astype(BF16)
        pm = pm_ref[0, :, mat, :].astype(BF16)
        y = _bmm(ry, gh) + _bmm(ry, gl) + yc_ref[0, :, tok, :]
        g0 = _bmm(pm, gh) + _bmm(pm, gl) + qm_ref[0, :, mat, :]
        mean = jnp.mean(y, axis=-1, keepdims=True)
        d = y - mean
        var = jnp.mean(d * d, axis=-1, keepdims=True)
        yn = d * lax.rsqrt(var + GN_EPS) * lnw_ref[...] + lnb_ref[...]
        o_ref[0, :, tok, :] = (yn + bon_ref[0, :, tok, :]) * g_ref[0, :, tok, :]
    state[...] = g0


def _rw_scan(ry, yc, pm, qm, g, bon, lnw, lnb):
    B, H, T, N = ry.shape
    C = RW_SCAN_CHUNKS * RW_CHUNK
    tok = pl.BlockSpec((1, H, C, N), lambda bi, c: (bi, 0, c, 0))
    mat = pl.BlockSpec((1, H, RW_SCAN_CHUNKS * N, N), lambda bi, c: (bi, 0, c, 0))
    par = pl.BlockSpec((H, 1, N), lambda bi, c: (0, 0, 0))
    return pl.pallas_call(
        _rw_scan_kernel,
        grid=(B, T // C),
        in_specs=[tok, tok, mat, mat, tok, tok, par, par],
        out_specs=tok,
        out_shape=jax.ShapeDtypeStruct((B, H, T, N), F32),
        scratch_shapes=[pltpu.VMEM((H, N, N), F32)],
        compiler_params=_cparams("parallel", "arbitrary"),
        name="rwkv_scan",
    )(ry, yc, pm, qm, g, bon, lnw, lnb)


def _out_kernel(x_ref, oa_ref, ob_ref, oc_ref, woa_ref, wob_ref, woc_ref, g1_ref, o_ref):
    acc = _dot(oa_ref[0].astype(BF16), woa_ref[...]) + _dot(oc_ref[0].astype(BF16), woc_ref[...])
    for h in range(B_HEADS):
        acc = acc + _dot(ob_ref[0, h].astype(BF16), wob_ref[h])
    o_ref[0] = x_ref[0] + g1_ref[0] * acc


def _out_proj(x, oa, ob, oc, woa, wob, woc, g1):
    B, T, D = x.shape
    tm = 512
    row = lambda b, i: (b, i, 0)
    fix2 = lambda b, i: (0, 0)
    return pl.pallas_call(
        _out_kernel,
        grid=(B, T // tm),
        in_specs=[pl.BlockSpec((1, tm, D), row),
                  pl.BlockSpec((1, tm, oa.shape[2]), row),
                  pl.BlockSpec((1, B_HEADS, tm, HEAD_DIM), lambda b, i: (b, 0, i, 0)),
                  pl.BlockSpec((1, tm, oc.shape[2]), row),
                  pl.BlockSpec(woa.shape, fix2),
                  pl.BlockSpec(wob.shape, lambda b, i: (0, 0, 0)),
                  pl.BlockSpec(woc.shape, fix2),
                  pl.BlockSpec((1, 1, D), lambda b, i: (b, 0, 0))],
        out_specs=pl.BlockSpec((1, tm, D), row),
        out_shape=jax.ShapeDtypeStruct((B, T, D), F32),
        compiler_params=_cparams("parallel", "parallel"),
        name="out_proj",
    )(x, oa, ob, oc, woa, wob, woc, g1.reshape(B, 1, D))


def _ffn_kernel(x_ref, g_ref, sc_ref, sh_ref, g2_ref, wg_ref, wu_ref, wo_ref, o_ref, h_s, acc_s):
    j = pl.program_id(2)

    @pl.when(j == 0)
    def _():
        x = x_ref[0]
        y = x * lax.rsqrt(jnp.mean(x * x, axis=-1, keepdims=True) + NORM_EPS) * g_ref[...]
        h_s[...] = (y * (1.0 + sc_ref[0]) + sh_ref[0]).astype(BF16)
        acc_s[...] = jnp.zeros_like(acc_s)

    h = h_s[...]
    gate = _dot(h, wg_ref[...])
    up = _dot(h, wu_ref[...])
    act = (gate * jax.nn.sigmoid(gate) * up).astype(BF16)
    acc_s[...] += _dot(act, wo_ref[...])

    @pl.when(j == pl.num_programs(2) - 1)
    def _():
        o_ref[0] = x_ref[0] + g2_ref[0] * acc_s[...]


def _ffn(x, g, sc, sh, g2, wi, wo):
    B, T, D = x.shape
    F = wo.shape[0]
    tm, th = 512, F // 2
    nh = F // th
    row = lambda b, i, j: (b, i, 0)
    per_b = lambda b, i, j: (b, 0, 0)
    return pl.pallas_call(
        _ffn_kernel,
        grid=(B, T // tm, nh),
        in_specs=[pl.BlockSpec((1, tm, D), row),
                  pl.BlockSpec((1, D), lambda b, i, j: (0, 0)),
                  pl.BlockSpec((1, 1, D), per_b),
                  pl.BlockSpec((1, 1, D), per_b),
                  pl.BlockSpec((1, 1, D), per_b),
                  pl.BlockSpec((D, th), lambda b, i, j: (0, j)),
                  pl.BlockSpec((D, th), lambda b, i, j: (0, j + nh)),
                  pl.BlockSpec((th, D), lambda b, i, j: (j, 0))],
        out_specs=pl.BlockSpec((1, tm, D), row),
        out_shape=jax.ShapeDtypeStruct((B, T, D), F32),
        scratch_shapes=[pltpu.VMEM((tm, D), BF16), pltpu.VMEM((tm, D), F32)],
        compiler_params=_cparams("parallel", "parallel", "arbitrary"),
        name="ffn_swiglu",
    )(x, g.reshape(1, D), sc.reshape(B, 1, D), sh.reshape(B, 1, D), g2.reshape(B, 1, D), wi, wi, wo)


def _rot_cols(w):
    half = HEAD_DIM // 2
    return jnp.concatenate([-w[..., half:], w[..., :half]], axis=-1)


def _rope_slot(w):
    return jnp.concatenate([w, _rot_cols(w)], axis=-1)


def _pad_slot(w):
    return jnp.pad(w, [(0, 0)] * (w.ndim - 1) + [(0, SLOT - w.shape[-1])])


def _perm_gain(g):
    half = HEAD_DIM // 2
    return jnp.concatenate([g, g[half:], g[:half]]).reshape(1, SLOT)


def _rope_table(pos):
    half = HEAD_DIM // 2
    inv = ROPE_THETA ** (-np.arange(half, dtype=np.float64) / half)
    ang = np.asarray(pos, np.float64)[:, None] * inv[None, :]
    cos, sin = np.cos(ang), np.sin(ang)
    return jnp.asarray(np.concatenate([cos, cos, sin, sin], axis=1), F32)


def _in_weights(w):
    D = w.shape[0]
    hd = HEAD_DIM
    o = 0
    qa = w[:, o:o + A_HEADS * hd]; o += A_HEADS * hd
    ka = w[:, o:o + hd]; o += hd
    va = w[:, o:o + hd]; o += hd
    iq = w[:, o:o + IDX_HEADS * hd]; o += IDX_HEADS * hd
    ik = w[:, o:o + hd]; o += hd
    iw = w[:, o:o + IDX_HEADS]; o += IDX_HEADS
    nb = 3 * B_WIDTH + DECAY_LORA + AAA_LORA + GATE_LORA
    wb = w[:, o:o + nb]; o += nb
    qc = w[:, o:o + C_HEADS * hd]; o += C_HEADS * hd
    kc, vc, ksl, vsl, kwn, vwn = (w[:, o + i * hd:o + (i + 1) * hd] for i in range(6)); o += 6 * hd
    gc = w[:, o:o + 3 * C_HEADS]
    heads = lambda m, n: [m[:, i * hd:(i + 1) * hd] for i in range(n)]
    wa = jnp.concatenate([_rope_slot(h) for h in heads(qa, A_HEADS)]
                         + [_rope_slot(h) for h in heads(iq, IDX_HEADS)]
                         + [_rope_slot(ka), _rope_slot(ik), _pad_slot(va), _pad_slot(iw)], axis=1)
    wc = jnp.concatenate([_rope_slot(h) for h in heads(qc, C_HEADS)]
                         + [_rope_slot(ksl), _rope_slot(kwn), jnp.concatenate([kc, vc], axis=1),
                            _pad_slot(vsl), _pad_slot(vwn), _pad_slot(gc)], axis=1)
    return wa.astype(BF16), wb.astype(BF16), wc.astype(BF16)


def _cmp_weights(pe, w1, w2):
    half = CMP_LEN // 2
    hd = HEAD_DIM
    zero = jnp.zeros((half, hd, CMP_HID), F32)

    def expand(lo):
        wk = w1[0].reshape(CMP_LEN, hd, CMP_HID)[lo:lo + half]
        wv = w1[1].reshape(CMP_LEN, hd, CMP_HID)[lo:lo + half]
        k_rows = jnp.concatenate([wk, zero], axis=1)
        v_rows = jnp.concatenate([zero, wv], axis=1)
        return jnp.concatenate([k_rows, v_rows], axis=2).reshape(half * SLOT, 2 * CMP_HID).astype(BF16)

    def pe_row(lo):
        return jnp.concatenate([pe[0, lo:lo + half], pe[1, lo:lo + half]], axis=1).reshape(1, half * SLOT)

    w2k = _rope_slot(w2[0]).astype(BF16)
    w2v = _pad_slot(w2[1]).astype(BF16)
    return pe_row(0), pe_row(half), expand(0), expand(half), w2k, w2v


def _nsa_tables(T):
    n_cmp = (T - CMP_LEN) // CMP_STRIDE + 1
    n_blk = T // SEL_LEN
    starts = np.arange(n_cmp) * CMP_STRIDE
    end_pos = starts + CMP_LEN - 1
    sel_start = np.arange(n_blk) * SEL_LEN
    ovl = np.zeros((SLOT, SLOT), np.float32)
    ovl[:n_cmp, :n_blk] = ((starts[:, None] <= sel_start[None, :] + SEL_LEN - 1)
                           & (end_pos[:, None] >= sel_start[None, :]))
    expand = np.zeros((SLOT, T), np.float32)
    expand[np.arange(T) // SEL_LEN, np.arange(T)] = 1.0
    m = T // CMP_STRIDE
    cse = _rope_table(np.arange(m) * CMP_STRIDE + CMP_LEN - 1)
    return n_cmp, jnp.asarray(ovl, BF16), jnp.asarray(expand, BF16), cse


def _out_weights(w):
    hd = HEAD_DIM
    D = w.shape[1]

    def slots(base, n):
        blocks = [jnp.concatenate([w[base + i * hd:base + (i + 1) * hd], jnp.zeros((SLOT - hd, D), F32)])
                  for i in range(n)]
        return jnp.concatenate(blocks).astype(BF16)

    a0 = 0
    b0 = A_HEADS * hd
    c0 = b0 + B_WIDTH
    return slots(a0, A_HEADS), w[b0:c0].reshape(B_HEADS, hd, D).astype(BF16), slots(c0, C_HEADS)


def kernel(x, c, ada_w, ada_b, norm1_g, w_in, dsa_q_g, dsa_k_g, rwkv_mu, rwkv_w0, rwkv_w2, rwkv_a0, rwkv_a2, rwkv_g2, rwkv_k_k, rwkv_k_a, rwkv_r_k, rwkv_ln_w, rwkv_ln_b, nsa_q_g, nsa_k_g, nsa_pe, nsa_w1, nsa_w2, w_out, norm2_g, ffn_wi, ffn_wo):
    B, T, D = x.shape
    L = w_in.shape[0]
    W = B_WIDTH
    assert T % 256 == 0 and T % RW_CHUNK == 0 and D % SLOT == 0
    mod = _modulation(c, ada_w, ada_b)
    cs = _rope_table(np.arange(T))
    n_cmp, ovl, expand, cse = _nsa_tables(T)
    bd = jnp.asarray(np.kron(np.eye(B_HEADS), np.ones((HEAD_DIM, HEAD_DIM))), BF16)
    lora_pad = jnp.zeros((SLOT - DECAY_LORA, W), F32)
    for l in range(L):
        sh1, sc1, g1, sh2, sc2, g2 = (mod[l, :, i * D:(i + 1) * D] for i in range(6))
        wa, wb, wc = _in_weights(w_in[l])
        za, zb, zc = _in_proj(x, norm1_g[l], sc1, sh1, wa, wb, wc)
        qa, iq, ka, ik, va, iw = _dsa_prep(za, cs, _perm_gain(dsa_q_g[l]), _perm_gain(dsa_k_g[l]))
        oa = _dsa(qa, iq, iw, ka, ik, va)
        w2e = jnp.concatenate([rwkv_w2[l], lora_pad]).astype(BF16)
        a2e = jnp.concatenate([lora_pad, rwkv_a2[l]]).astype(BF16)
        row = lambda p: p.reshape(1, W)
        r, k2, v, kk, bb, lw, gg, bon = _rw_pre(
            zb, rwkv_mu[l].reshape(1, -1), row(rwkv_w0[l]), w2e, row(rwkv_a0[l]), a2e,
            rwkv_g2[l].astype(BF16), row(rwkv_k_k[l]), row(rwkv_k_a[l]), row(rwkv_r_k[l]), bd)
        ry, yc, pm, qm = _rw_chunks(r, k2, v, kk, bb, lw)
        ob = _rw_scan(ry, yc, pm, qm, gg, bon, rwkv_ln_w[l].reshape(B_HEADS, 1, HEAD_DIM),
                      rwkv_ln_b[l].reshape(B_HEADS, 1, HEAD_DIM))
        gkc = _perm_gain(nsa_k_g[l])
        qc, ks, kw, vs, vw, gc = _nsa_prep(zc, cs, _perm_gain(nsa_q_g[l]), gkc)
        kv_slot = (C_HEADS + 2) * SLOT
        rm = zc[:, :, kv_slot:kv_slot + SLOT].reshape(B, T // CMP_STRIDE, CMP_STRIDE * SLOT)
        kc, vc = _nsa_compress(rm, *_cmp_weights(nsa_pe[l], nsa_w1[l], nsa_w2[l]), cse, gkc)
        oc = _nsa(qc, gc, kc, vc, ks, vs, kw, vw, ovl, expand, n_cmp)
        woa, wob, woc = _out_weights(w_out[l])
        x = _out_proj(x, oa, ob, oc, woa, wob, woc, g1)
        x = _ffn(x, norm2_g[l], sc2, sh2, g2, ffn_wi[l].astype(BF16), ffn_wo[l].astype(BF16))
    return x
```

```python
import functools

import numpy as np
import jax
import jax.numpy as jnp
from jax import lax
from jax.experimental import pallas as pl
from jax.experimental.pallas import tpu as pltpu

F32 = jnp.float32
BF16 = jnp.bfloat16

HEAD_DIM = 64
SLOT = 128
A_HEADS = 4
IDX_HEADS = 4
DSA_TOPK = 256
B_HEADS = 8
B_WIDTH = B_HEADS * HEAD_DIM
DECAY_LORA = 64
AAA_LORA = 64
GATE_LORA = 128
GN_EPS = 64e-5
C_HEADS = 4
CMP_LEN = 32
CMP_STRIDE = 16
CMP_HID = 256
SEL_LEN = 64
SEL_TOPN = 16
WIN = 512
Q_BLOCK = 128
ROPE_THETA = 10000.0
NORM_EPS = 1e-6
NEG = -1e30
INT_MIN = -2147483648
RW_CHUNK = 64
RW_SCAN_CHUNKS = 4
PREFIX_GROUPS = 16
VMEM_LIMIT_BYTES = 56 * 1024 * 1024


def _cparams(*sem):
    return pltpu.CompilerParams(dimension_semantics=sem, vmem_limit_bytes=VMEM_LIMIT_BYTES)


def _dot(a, b):
    return jnp.dot(a, b, preferred_element_type=F32)


def _dot_nt(a, b):
    return lax.dot_general(a, b, (((1,), (1,)), ((), ())), preferred_element_type=F32)


def _split3(x):
    hi = x.astype(BF16)
    r1 = x - hi.astype(F32)
    mid = r1.astype(BF16)
    lo = (r1 - mid.astype(F32)).astype(BF16)
    return hi, mid, lo


def _dot3(x, w):
    hi, mid, lo = _split3(x)
    return _dot(hi, w) + _dot(mid, w) + _dot(lo, w)


def _bmm(a, b):
    return jnp.einsum('hts,hsd->htd', a, b, preferred_element_type=F32)


def _bmm_nt(a, b):
    return jnp.einsum('htj,hsj->hts', a, b, preferred_element_type=F32)


def _bmm_tn(a, b):
    return jnp.einsum('htj,htd->hjd', a, b, preferred_element_type=F32)


def _mod_kernel(c_ref, w_ref, b_ref, o_ref):
    c = c_ref[...]
    s = (c * jax.nn.sigmoid(c)).astype(BF16)
    o_ref[0] = _dot(s, w_ref[0]) + b_ref[0]


def _modulation(c, ada_w, ada_b):
    L, D, N = ada_w.shape
    B = c.shape[0]
    tn = 1536
    return pl.pallas_call(
        _mod_kernel,
        grid=(L, N // tn),
        in_specs=[pl.BlockSpec((B, D), lambda l, j: (0, 0)),
                  pl.BlockSpec((1, D, tn), lambda l, j: (l, 0, j)),
                  pl.BlockSpec((1, 1, tn), lambda l, j: (l, 0, j))],
        out_specs=pl.BlockSpec((1, B, tn), lambda l, j: (l, 0, j)),
        out_shape=jax.ShapeDtypeStruct((L, B, N), F32),
        compiler_params=_cparams("parallel", "parallel"),
        name="adaln_modulation",
    )(c, ada_w.astype(BF16), ada_b.reshape(L, 1, N))


def _in_kernel(x_ref, g_ref, sc_ref, sh_ref, wa_ref, wb_ref, wc_ref, za_ref, zb_ref, zc_ref):
    x = x_ref[0]
    y = x * lax.rsqrt(jnp.mean(x * x, axis=-1, keepdims=True) + NORM_EPS) * g_ref[...]
    h = (y * (1.0 + sc_ref[0]) + sh_ref[0]).astype(BF16)
    za_ref[0] = _dot(h, wa_ref[...])
    zb_ref[0] = _dot(h, wb_ref[...])
    zc_ref[0] = _dot(h, wc_ref[...])


def _in_proj(x, g, sc, sh, wa, wb, wc):
    B, T, D = x.shape
    tm = 256
    na, nb, nc = wa.shape[1], wb.shape[1], wc.shape[1]
    row = lambda b, i: (b, i, 0)
    fix = lambda b, i: (0, 0)
    per_b = lambda b, i: (b, 0, 0)
    return pl.pallas_call(
        _in_kernel,
        grid=(B, T // tm),
        in_specs=[pl.BlockSpec((1, tm, D), row),
                  pl.BlockSpec((1, D), fix),
                  pl.BlockSpec((1, 1, D), per_b),
                  pl.BlockSpec((1, 1, D), per_b),
                  pl.BlockSpec((D, na), fix),
                  pl.BlockSpec((D, nb), fix),
                  pl.BlockSpec((D, nc), fix)],
        out_specs=[pl.BlockSpec((1, tm, na), row),
                   pl.BlockSpec((1, tm, nb), row),
                   pl.BlockSpec((1, tm, nc), row)],
        out_shape=[jax.ShapeDtypeStruct((B, T, na), F32),
                   jax.ShapeDtypeStruct((B, T, nb), F32),
                   jax.ShapeDtypeStruct((B, T, nc), F32)],
        compiler_params=_cparams("parallel", "parallel"),
        name="in_proj",
    )(x, g.reshape(1, D), sc.reshape(B, 1, D), sh.reshape(B, 1, D), wa, wb, wc)


def _norm_rope(x, table):
    rs = lax.rsqrt(jnp.mean(x * x, axis=-1, keepdims=True) + NORM_EPS)
    return x * rs * table


def _fold(y):
    return y + pltpu.roll(y, HEAD_DIM, 1)


def _dsa_prep_kernel(za_ref, cs_ref, gq_ref, gk_ref, qa_ref, iq_ref, ka_ref, ik_ref, va_ref, iw_ref):
    cs = cs_ref[...]
    tq = cs * gq_ref[...] * (HEAD_DIM ** -0.5)
    tk = cs * gk_ref[...]
    for h in range(A_HEADS):
        sl = slice(h * SLOT, (h + 1) * SLOT)
        qa_ref[0, :, sl] = _norm_rope(za_ref[0, :, sl], tq).astype(BF16)
    for h in range(IDX_HEADS):
        sl = slice(h * SLOT, (h + 1) * SLOT)
        src = slice((A_HEADS + h) * SLOT, (A_HEADS + h + 1) * SLOT)
        iq_ref[0, :, sl] = (za_ref[0, :, src] * cs).astype(BF16)
    base = (A_HEADS + IDX_HEADS) * SLOT
    ka_ref[0] = _fold(_norm_rope(za_ref[0, :, base:base + SLOT], tk)).astype(BF16)
    ik_ref[0] = _fold(za_ref[0, :, base + SLOT:base + 2 * SLOT] * cs).astype(BF16)
    va_ref[0] = za_ref[0, :, base + 2 * SLOT:base + 3 * SLOT].astype(BF16)
    iw_ref[0] = za_ref[0, :, base + 3 * SLOT:base + 4 * SLOT]


def _dsa_prep(za, cs, gq, gk):
    B, T, NA = za.shape
    tm = 256
    row = lambda b, i: (b, i, 0)
    fix = lambda b, i: (0, 0)
    outs = [(A_HEADS * SLOT, BF16), (IDX_HEADS * SLOT, BF16), (SLOT, BF16), (SLOT, BF16), (SLOT, BF16),
            (SLOT, F32)]
    return pl.pallas_call(
        _dsa_prep_kernel,
        grid=(B, T // tm),
        in_specs=[pl.BlockSpec((1, tm, NA), row),
                  pl.BlockSpec((tm, SLOT), lambda b, i: (i, 0)),
                  pl.BlockSpec((1, SLOT), fix),
                  pl.BlockSpec((1, SLOT), fix)],
        out_specs=[pl.BlockSpec((1, tm, w), row) for w, _ in outs],
        out_shape=[jax.ShapeDtypeStruct((B, T, w), dt) for w, dt in outs],
        compiler_params=_cparams("parallel", "parallel"),
        name="dsa_prep",
    )(za, cs, gq, gk)


def _masked_attend(q_ref, k, v, mask, heads):
    for h in range(heads):
        sl = slice(h * SLOT, (h + 1) * SLOT)
        s = jnp.where(mask, _dot_nt(q_ref[0, :, sl], k), NEG)
        m = jnp.max(s, axis=1, keepdims=True)
        p = jnp.exp(s - m)
        l = jnp.sum(p, axis=1, keepdims=True)
        yield h, _dot(p.astype(BF16), v) / l


def _dsa_block(qa_ref, iq_ref, iw_ref, ka_ref, ik_ref, va_ref, o_ref, t0, Te, topk):
    rows = t0 + lax.broadcasted_iota(jnp.int32, (Q_BLOCK, 1), 0)
    cols = lax.broadcasted_iota(jnp.int32, (1, Te), 1)
    causal = cols <= rows
    if Te <= topk:
        mask = causal
    else:
        ik = ik_ref[0, :Te, :]
        iw = iw_ref[0]
        score = jnp.zeros((Q_BLOCK, Te), F32)
        for h in range(IDX_HEADS):
            s = _dot_nt(iq_ref[0, :, h * SLOT:(h + 1) * SLOT], ik)
            score = score + iw[:, h:h + 1] * jnp.maximum(s, 0.0)
        score = jnp.where(score == 0.0, 0.0, score)
        bits = lax.bitcast_convert_type(score, jnp.int32)
        key = bits ^ ((bits >> 31) & jnp.int32(0x7FFFFFFF))
        key = jnp.where(causal, key, jnp.int32(INT_MIN))
        kf = float(topk)

        def count_ge(cand):
            return jnp.sum(jnp.where(key >= cand, 1.0, 0.0), axis=1, keepdims=True)

        thr0 = jnp.where(count_ge(jnp.int32(0)) >= kf, jnp.int32(0), jnp.int32(INT_MIN))

        def body(i, thr):
            cand = thr | jnp.left_shift(jnp.int32(1), 30 - i)
            return jnp.where(count_ge(cand) >= kf, cand, thr)

        thr = lax.fori_loop(0, 31, body, thr0)
        gt = key > thr
        eq = key == thr
        need = kf - jnp.sum(jnp.where(gt, 1.0, 0.0), axis=1, keepdims=True)
        ri = lax.broadcasted_iota(jnp.int32, (SLOT, SLOT), 0)
        ci = lax.broadcasted_iota(jnp.int32, (SLOT, SLOT), 1)
        upper = jnp.where(ri <= ci, 1.0, 0.0).astype(BF16)
        run = jnp.zeros((Q_BLOCK, 1), F32)
        sel = []
        for c in range(Te // SLOT):
            sl = slice(c * SLOT, (c + 1) * SLOT)
            e = jnp.where(eq[:, sl], 1.0, 0.0)
            incl = _dot(e.astype(BF16), upper)
            sel.append(gt[:, sl] | (eq[:, sl] & (incl - e + run < need)))
            run = run + incl[:, SLOT - 1:SLOT]
        mask = jnp.concatenate(sel, axis=1) & causal
    for h, o in _masked_attend(qa_ref, ka_ref[0, :Te, :], va_ref[0, :Te, :], mask, A_HEADS):
        o_ref[0, :, h * SLOT:(h + 1) * SLOT] = o


def _dsa_kernel(qa_ref, iq_ref, iw_ref, ka_ref, ik_ref, va_ref, o_ref, *, T, topk, groups):
    qi = pl.program_id(1)
    per = (T // Q_BLOCK) // groups
    for g in range(groups):
        @pl.when(qi // per == g)
        def _(g=g):
            _dsa_block(qa_ref, iq_ref, iw_ref, ka_ref, ik_ref, va_ref, o_ref, qi * Q_BLOCK,
                       (g + 1) * per * Q_BLOCK, topk)


def _dsa(qa, iq, iw, ka, ik, va):
    B, T, _ = qa.shape
    topk = min(DSA_TOPK, T // 4)
    blk = lambda b, i: (b, i, 0)
    full = lambda b, i: (b, 0, 0)
    return pl.pallas_call(
        functools.partial(_dsa_kernel, T=T, topk=topk, groups=PREFIX_GROUPS),
        grid=(B, T // Q_BLOCK),
        in_specs=[pl.BlockSpec((1, Q_BLOCK, A_HEADS * SLOT), blk),
                  pl.BlockSpec((1, Q_BLOCK, IDX_HEADS * SLOT), blk),
                  pl.BlockSpec((1, Q_BLOCK, SLOT), blk),
                  pl.BlockSpec((1, T, SLOT), full),
                  pl.BlockSpec((1, T, SLOT), full),
                  pl.BlockSpec((1, T, SLOT), full)],
        out_specs=pl.BlockSpec((1, Q_BLOCK, A_HEADS * SLOT), blk),
        out_shape=jax.ShapeDtypeStruct((B, T, A_HEADS * SLOT), F32),
        compiler_params=_cparams("parallel", "parallel"),
        name="dsa_attention",
    )(qa, iq, iw, ka, ik, va)


def _nsa_prep_kernel(zc_ref, cs_ref, gq_ref, gk_ref, qc_ref, ks_ref, kw_ref, vs_ref, vw_ref, gc_ref):
    cs = cs_ref[...]
    tq = cs * gq_ref[...] * (HEAD_DIM ** -0.5)
    tk = cs * gk_ref[...]
    for h in range(C_HEADS):
        sl = slice(h * SLOT, (h + 1) * SLOT)
        qc_ref[0, :, sl] = _norm_rope(zc_ref[0, :, sl], tq).astype(BF16)
    base = C_HEADS * SLOT
    ks_ref[0] = _fold(_norm_rope(zc_ref[0, :, base:base + SLOT], tk)).astype(BF16)
    kw_ref[0] = _fold(_norm_rope(zc_ref[0, :, base + SLOT:base + 2 * SLOT], tk)).astype(BF16)
    vs_ref[0] = zc_ref[0, :, base + 3 * SLOT:base + 4 * SLOT].astype(BF16)
    vw_ref[0] = zc_ref[0, :, base + 4 * SLOT:base + 5 * SLOT].astype(BF16)
    gc_ref[0] = zc_ref[0, :, base + 5 * SLOT:base + 6 * SLOT]


def _nsa_prep(zc, cs, gq, gk):
    B, T, NC = zc.shape
    tm = 256
    row = lambda b, i: (b, i, 0)
    fix = lambda b, i: (0, 0)
    outs = [(C_HEADS * SLOT, BF16), (SLOT, BF16), (SLOT, BF16), (SLOT, BF16), (SLOT, BF16), (SLOT, F32)]
    return pl.pallas_call(
        _nsa_prep_kernel,
        grid=(B, T // tm),
        in_specs=[pl.BlockSpec((1, tm, NC), row),
                  pl.BlockSpec((tm, SLOT), lambda b, i: (i, 0)),
                  pl.BlockSpec((1, SLOT), fix),
                  pl.BlockSpec((1, SLOT), fix)],
        out_specs=[pl.BlockSpec((1, tm, w), row) for w, _ in outs],
        out_shape=[jax.ShapeDtypeStruct((B, T, w), dt) for w, dt in outs],
        compiler_params=_cparams("parallel", "parallel"),
        name="nsa_prep",
    )(zc, cs, gq, gk)


def _nsa_cmp_kernel(rm_ref, pet_ref, peb_ref, w1t_ref, w1b_ref, w2k_ref, w2v_ref, cse_ref, gk_ref,
                    kc_ref, vc_ref):
    rm = rm_ref[0]
    p = _dot((rm + pet_ref[...]).astype(BF16), w1t_ref[...])
    q = _dot((rm + peb_ref[...]).astype(BF16), w1b_ref[...])
    nrow = rm.shape[0]
    hid = jax.nn.gelu(p + pltpu.roll(q, nrow - 1, 0), approximate=True)
    kc = _dot(hid[:, :CMP_HID].astype(BF16), w2k_ref[...])
    kc_ref[0] = _fold(_norm_rope(kc, cse_ref[...] * gk_ref[...])).astype(BF16)
    vc_ref[0] = _dot(hid[:, CMP_HID:].astype(BF16), w2v_ref[...]).astype(BF16)


def _nsa_compress(rm, pet, peb, w1t, w1b, w2k, w2v, cse, gk):
    B, M, W = rm.shape
    fix = lambda b: (0, 0)
    return pl.pallas_call(
        _nsa_cmp_kernel,
        grid=(B,),
        in_specs=[pl.BlockSpec((1, M, W), lambda b: (b, 0, 0)),
                  pl.BlockSpec((1, W), fix), pl.BlockSpec((1, W), fix),
                  pl.BlockSpec(w1t.shape, fix), pl.BlockSpec(w1b.shape, fix),
                  pl.BlockSpec(w2k.shape, fix), pl.BlockSpec(w2v.shape, fix),
                  pl.BlockSpec(cse.shape, fix), pl.BlockSpec((1, SLOT), fix)],
        out_specs=[pl.BlockSpec((1, M, SLOT), lambda b: (b, 0, 0)),
                   pl.BlockSpec((1, M, SLOT), lambda b: (b, 0, 0))],
        out_shape=[jax.ShapeDtypeStruct((B, M, SLOT), BF16), jax.ShapeDtypeStruct((B, M, SLOT), BF16)],
        compiler_params=_cparams("parallel"),
        name="nsa_compress",
    )(rm, pet, peb, w1t, w1b, w2k, w2v, cse, gk)


def _nsa_kernel(qc_ref, gc_ref, kc_ref, vc_ref, ks_ref, vs_ref, kw_ref, vw_ref, ovl_ref, exp_ref, o_ref,
                slc_s, *, T, n_cmp, n_blk, top_n, win_len, groups):
    t0 = pl.program_id(1) * Q_BLOCK
    rows = t0 + lax.broadcasted_iota(jnp.int32, (Q_BLOCK, 1), 0)
    lane = lax.broadcasted_iota(jnp.int32, (1, SLOT), 1)
    is_cmp = lane < n_cmp
    cvalid = is_cmp & (lane * CMP_STRIDE + (CMP_LEN - 1) <= rows)
    kc = kc_ref[0]
    vc = vc_ref[0]
    psum = jnp.zeros((Q_BLOCK, SLOT), F32)
    o_cmp = []
    for h in range(C_HEADS):
        s = _dot_nt(qc_ref[0, :, h * SLOT:(h + 1) * SLOT], kc)
        s = jnp.where(is_cmp, jnp.where(cvalid, s, NEG), -jnp.inf)
        e = jnp.exp(s - jnp.max(s, axis=1, keepdims=True))
        p = jnp.where(cvalid, e / jnp.sum(e, axis=1, keepdims=True), 0.0)
        psum = psum + p
        o_cmp.append(_dot(p.astype(BF16), vc))
    imp = _dot3(psum, ovl_ref[...])
    cur = rows >> 6
    adm = lane * SEL_LEN <= rows
    forced = (lane == 0) | (lane == cur) | (lane == cur - 1)
    val = jnp.where(adm, jnp.where(forced, jnp.inf, imp), -jnp.inf)
    val = jnp.where(lane < n_blk, val, -jnp.inf)
    rank = jnp.zeros((Q_BLOCK, SLOT), F32)
    for j in range(n_blk):
        c = val[:, j:j + 1]
        rank = rank + jnp.where((c > val) | ((c == val) & (lane > j)), 1.0, 0.0)
    sel = jnp.where((rank < float(top_n)) & (lane < n_blk), 1.0, 0.0).astype(BF16)
    qi = pl.program_id(1)
    per = (T // Q_BLOCK) // groups
    for gi in range(groups):
        @pl.when(qi // per == gi)
        def _(gi=gi):
            te = (gi + 1) * per * Q_BLOCK
            cols = lax.broadcasted_iota(jnp.int32, (1, te), 1)
            mask = (_dot(sel, exp_ref[:, :te]) > 0.5) & (cols <= rows)
            for h, o in _masked_attend(qc_ref, ks_ref[0, :te, :], vs_ref[0, :te, :], mask, C_HEADS):
                slc_s[:, h * SLOT:(h + 1) * SLOT] = o
    ws = pl.multiple_of(jnp.clip(t0 - WIN, 0, T - win_len), Q_BLOCK)
    wpos = ws + lax.broadcasted_iota(jnp.int32, (1, win_len), 1)
    wmask = (wpos <= rows) & (wpos > rows - WIN)
    kw = kw_ref[0, pl.ds(ws, win_len), :]
    vw = vw_ref[0, pl.ds(ws, win_len), :]
    o_win = [o for _, o in _masked_attend(qc_ref, kw, vw, wmask, C_HEADS)]
    g = jax.nn.sigmoid(gc_ref[0])
    for h in range(C_HEADS):
        o_ref[0, :, h * SLOT:(h + 1) * SLOT] = (g[:, 3 * h:3 * h + 1] * o_cmp[h]
                                                + g[:, 3 * h + 1:3 * h + 2] * slc_s[:, h * SLOT:(h + 1) * SLOT]
                                                + g[:, 3 * h + 2:3 * h + 3] * o_win[h])


def _nsa(qc, gc, kc, vc, ks, vs, kw, vw, ovl, expand, n_cmp):
    B, T, _ = qc.shape
    n_blk = T // SEL_LEN
    blk = lambda b, i: (b, i, 0)
    full = lambda b, i: (b, 0, 0)
    fix = lambda b, i: (0, 0)
    M = kc.shape[1]
    kern = functools.partial(_nsa_kernel, T=T, n_cmp=n_cmp, n_blk=n_blk, top_n=min(SEL_TOPN, n_blk),
                             win_len=min(WIN + Q_BLOCK, T), groups=PREFIX_GROUPS)
    return pl.pallas_call(
        kern,
        grid=(B, T // Q_BLOCK),
        in_specs=[pl.BlockSpec((1, Q_BLOCK, C_HEADS * SLOT), blk),
                  pl.BlockSpec((1, Q_BLOCK, SLOT), blk),
                  pl.BlockSpec((1, M, SLOT), full),
                  pl.BlockSpec((1, M, SLOT), full),
                  pl.BlockSpec((1, T, SLOT), full),
                  pl.BlockSpec((1, T, SLOT), full),
                  pl.BlockSpec((1, T, SLOT), full),
                  pl.BlockSpec((1, T, SLOT), full),
                  pl.BlockSpec(ovl.shape, fix),
                  pl.BlockSpec(expand.shape, fix)],
        out_specs=pl.BlockSpec((1, Q_BLOCK, C_HEADS * SLOT), blk),
        out_shape=jax.ShapeDtypeStruct((B, T, C_HEADS * SLOT), F32),
        scratch_shapes=[pltpu.VMEM((Q_BLOCK, C_HEADS * SLOT), F32)],
        compiler_params=_cparams("parallel", "parallel"),
        name="nsa_attention",
    )(qc, gc, kc, vc, ks, vs, kw, vw, ovl, expand)


def _rw_pre_kernel(zb_ref, mu_ref, w0_ref, w2_ref, a0_ref, a2_ref, g2_ref, kk_ref, ka_ref, rk_ref, bd_ref,
                   r_o, k_o, v_o, kk_o, b_o, lw_o, g_o, bon_o, carry):
    @pl.when(pl.program_id(1) == 0)
    def _():
        carry[...] = jnp.zeros_like(carry)

    zt = zb_ref[0]
    tm = zt.shape[0]
    row0 = lax.broadcasted_iota(jnp.int32, (tm, 1), 0) == 0
    prev = jnp.where(row0, carry[7:8, :], pltpu.roll(zt, 1, 0))
    carry[...] = zt[tm - 8:tm, :]
    z = zt + (prev - zt) * mu_ref[...]
    W = B_WIDTH
    r, k, v = z[:, :W], z[:, W:2 * W], z[:, 2 * W:3 * W]
    lora = z[:, 3 * W:3 * W + SLOT]
    gd = z[:, 3 * W + SLOT:3 * W + SLOT + GATE_LORA]
    wl = w0_ref[...] + _dot(jnp.tanh(lora).astype(BF16), w2_ref[...])
    w_log = -jax.nn.softplus(-wl) - 0.5
    lw = -jnp.exp(w_log)
    a = jax.nn.sigmoid(a0_ref[...] + _dot(lora.astype(BF16), a2_ref[...]))
    g = _dot(jax.nn.sigmoid(gd).astype(BF16), g2_ref[...])
    kk = k * kk_ref[...]
    bd = bd_ref[...]
    kk = kk / jnp.maximum(jnp.sqrt(_dot3(kk * kk, bd)), 1e-12)
    k2 = k * (1.0 + (a - 1.0) * ka_ref[...])
    bon = _dot3(r * k2 * rk_ref[...], bd) * v
    outs = ((r_o, r), (k_o, k2), (v_o, v), (kk_o, kk), (b_o, kk * a), (lw_o, lw), (g_o, g), (bon_o, bon))
    for ref, val in outs:
        for h in range(B_HEADS):
            ref[0, h] = val[:, h * HEAD_DIM:(h + 1) * HEAD_DIM]


def _rw_pre(zb, mu, w0, w2e, a0, a2e, g2, k_k, k_a, r_k, bd):
    B, T, NB = zb.shape
    tm = 256
    W = B_WIDTH
    fix = lambda b, i: (0, 0)
    hm = pl.BlockSpec((1, B_HEADS, tm, HEAD_DIM), lambda b, i: (b, 0, i, 0))
    return pl.pallas_call(
        _rw_pre_kernel,
        grid=(B, T // tm),
        in_specs=[pl.BlockSpec((1, tm, NB), lambda b, i: (b, i, 0)),
                  pl.BlockSpec((1, NB), fix),
                  pl.BlockSpec((1, W), fix), pl.BlockSpec((SLOT, W), fix),
                  pl.BlockSpec((1, W), fix), pl.BlockSpec((SLOT, W), fix),
                  pl.BlockSpec((GATE_LORA, W), fix),
                  pl.BlockSpec((1, W), fix), pl.BlockSpec((1, W), fix), pl.BlockSpec((1, W), fix),
                  pl.BlockSpec((W, W), fix)],
        out_specs=[hm] * 8,
        out_shape=[jax.ShapeDtypeStruct((B, B_HEADS, T, HEAD_DIM), F32)] * 8,
        scratch_shapes=[pltpu.VMEM((8, NB), F32)],
        compiler_params=_cparams("parallel", "arbitrary"),
        name="rwkv_prep",
    )(zb, mu, w0, w2e, a0, a2e, g2, k_k, k_a, r_k, bd)


def _rw_chunk_kernel(r_ref, k_ref, v_ref, kk_ref, b_ref, lw_ref, ry_o, yc_o, pm_o, qm_o):
    r, k, v, kk, b, lw = (x[0] for x in (r_ref, k_ref, v_ref, kk_ref, b_ref, lw_ref))
    H, C, N = r.shape
    ti = lax.broadcasted_iota(jnp.int32, (H, C, C), 1)
    si = lax.broadcasted_iota(jnp.int32, (H, C, C), 2)
    tri = jnp.where(si <= ti, 1.0, 0.0).astype(BF16)
    h1, h2, _ = _split3(lw)
    cum = _bmm(tri, h1) + _bmm(tri, h2)
    tot = cum[:, C - 1:C, :]
    e_in = jnp.exp(cum)
    e_out = jnp.exp(-cum)
    a_s = (kk * jnp.exp(cum - lw)).astype(BF16)
    r_s = r * e_in
    r_b = r_s.astype(BF16)
    b_s = (b * e_out).astype(BF16)
    k_s = (k * e_out).astype(BF16)
    e_end = jnp.exp(tot - cum)
    b_e = (b * e_end).astype(BF16)
    k_e = (k * e_end).astype(BF16)
    vb = v.astype(BF16)
    low = si < ti
    lowi = si <= ti
    l_ab = jnp.where(low, _bmm_nt(a_s, b_s), 0.0)
    l_ak = jnp.where(low, _bmm_nt(a_s, k_s), 0.0).astype(BF16)
    m_rb = jnp.where(lowi, _bmm_nt(r_b, b_s), 0.0).astype(BF16)
    m_rk = jnp.where(lowi, _bmm_nt(r_b, k_s), 0.0).astype(BF16)
    inv = jnp.where(si == ti, 1.0, 0.0) - jnp.where(((ti % 2) == 1) & (si == ti - 1), l_ab, 0.0)
    m = 2
    while m < C:
        pair = (((ti // m) % 2) == 1) & ((si // m) == (ti // m) - 1)
        lm = jnp.where(pair, l_ab, 0.0).astype(BF16)
        ib = inv.astype(BF16)
        inv = inv - _bmm(_bmm(ib, lm).astype(BF16), ib)
        m *= 2
    ib = inv.astype(BF16)
    w1 = _bmm(ib, a_s).astype(BF16)
    w2 = _bmm(ib, _bmm(l_ak, vb).astype(BF16)).astype(BF16)
    ry_o[0] = r_s - _bmm(m_rb, w1)
    yc_o[0] = _bmm(m_rk, vb) - _bmm(m_rb, w2)
    ji = lax.broadcasted_iota(jnp.int32, (H, N, N), 1)
    di = lax.broadcasted_iota(jnp.int32, (H, N, N), 2)
    decay = jnp.where(ji == di, jnp.broadcast_to(jnp.exp(tot), (H, N, N)), 0.0)
    pm_o[0] = decay - _bmm_tn(b_e, w1)
    qm_o[0] = _bmm_tn(k_e, vb) - _bmm_tn(b_e, w2)


def _rw_chunks(r, k, v, kk, b, lw):
    B, H, T, N = r.shape
    C = RW_CHUNK
    nc = T // C
    tok = pl.BlockSpec((1, H, C, N), lambda bi, c: (bi, 0, c, 0))
    return pl.pallas_call(
        _rw_chunk_kernel,
        grid=(B, nc),
        in_specs=[tok] * 6,
        out_specs=[tok, tok, tok, tok],
        out_shape=[jax.ShapeDtypeStruct((B, H, T, N), F32)] * 2
        + [jax.ShapeDtypeStruct((B, H, nc * N, N), F32)] * 2,
        compiler_params=_cparams("parallel", "parallel"),
        name="rwkv_chunk_ops",
    )(r, k, v, kk, b, lw)


def _rw_scan_kernel(ry_ref, yc_ref, pm_ref, qm_ref, g_ref, bon_ref, lnw_ref, lnb_ref, o_ref, state):
    @pl.when(pl.program_id(1) == 0)
    def _():
        state[...] = jnp.zeros_like(state)

    C = N = RW_CHUNK
    g0 = state[...]
    for i in range(ry_ref.shape[2] // C):
        tok = slice(i * C, (i + 1) * C)
        mat = slice(i * N, (i + 1) * N)
        gh = g0.astype(BF16)
        gl = (g0 - gh.astype(F32)).astype(BF16)
        ry = ry_ref[0, :, tok, :].astype(BF16)
        pm = pm_ref[0, :, mat, :].astype(BF16)
        y = _bmm(ry, gh) + _bmm(ry, gl) + yc_ref[0, :, tok, :]
        g0 = _bmm(pm, gh) + _bmm(pm, gl) + qm_ref[0, :, mat, :]
        mean = jnp.mean(y, axis=-1, keepdims=True)
        d = y - mean
        var = jnp.mean(d * d, axis=-1, keepdims=True)
        yn = d * lax.rsqrt(var + GN_EPS) * lnw_ref[...] + lnb_ref[...]
        o_ref[0, :, tok, :] = (yn + bon_ref[0, :, tok, :]) * g_ref[0, :, tok, :]
    state[...] = g0


def _rw_scan(ry, yc, pm, qm, g, bon, lnw, lnb):
    B, H, T, N = ry.shape
    C = RW_SCAN_CHUNKS * RW_CHUNK
    tok = pl.BlockSpec((1, H, C, N), lambda bi, c: (bi, 0, c, 0))
    mat = pl.BlockSpec((1, H, RW_SCAN_CHUNKS * N, N), lambda bi, c: (bi, 0, c, 0))
    par = pl.BlockSpec((H, 1, N), lambda bi, c: (0, 0, 0))
    return pl.pallas_call(
        _rw_scan_kernel,
        grid=(B, T // C),
        in_specs=[tok, tok, mat, mat, tok, tok, par, par],
        out_specs=tok,
        out_shape=jax.ShapeDtypeStruct((B, H, T, N), F32),
        scratch_shapes=[pltpu.VMEM((H, N, N), F32)],
        compiler_params=_cparams("parallel", "arbitrary"),
        name="rwkv_scan",
    )(ry, yc, pm, qm, g, bon, lnw, lnb)


def _out_kernel(x_ref, oa_ref, ob_ref, oc_ref, woa_ref, wob_ref, woc_ref, g1_ref, o_ref):
    acc = _dot(oa_ref[0].astype(BF16), woa_ref[...]) + _dot(oc_ref[0].astype(BF16), woc_ref[...])
    for h in range(B_HEADS):
        acc = acc + _dot(ob_ref[0, h].astype(BF16), wob_ref[h])
    o_ref[0] = x_ref[0] + g1_ref[0] * acc


def _out_proj(x, oa, ob, oc, woa, wob, woc, g1):
    B, T, D = x.shape
    tm = 512
    row = lambda b, i: (b, i, 0)
    fix2 = lambda b, i: (0, 0)
    return pl.pallas_call(
        _out_kernel,
        grid=(B, T // tm),
        in_specs=[pl.BlockSpec((1, tm, D), row),
                  pl.BlockSpec((1, tm, oa.shape[2]), row),
                  pl.BlockSpec((1, B_HEADS, tm, HEAD_DIM), lambda b, i: (b, 0, i, 0)),
                  pl.BlockSpec((1, tm, oc.shape[2]), row),
                  pl.BlockSpec(woa.shape, fix2),
                  pl.BlockSpec(wob.shape, lambda b, i: (0, 0, 0)),
                  pl.BlockSpec(woc.shape, fix2),
                  pl.BlockSpec((1, 1, D), lambda b, i: (b, 0, 0))],
        out_specs=pl.BlockSpec((1, tm, D), row),
        out_shape=jax.ShapeDtypeStruct((B, T, D), F32),
        compiler_params=_cparams("parallel", "parallel"),
        name="out_proj",
    )(x, oa, ob, oc, woa, wob, woc, g1.reshape(B, 1, D))


def _ffn_kernel(x_ref, g_ref, sc_ref, sh_ref, g2_ref, wg_ref, wu_ref, wo_ref, o_ref, h_s, acc_s):
    j = pl.program_id(2)

    @pl.when(j == 0)
    def _():
        x = x_ref[0]
        y = x * lax.rsqrt(jnp.mean(x * x, axis=-1, keepdims=True) + NORM_EPS) * g_ref[...]
        h_s[...] = (y * (1.0 + sc_ref[0]) + sh_ref[0]).astype(BF16)
        acc_s[...] = jnp.zeros_like(acc_s)

    h = h_s[...]
    gate = _dot(h, wg_ref[...])
    up = _dot(h, wu_ref[...])
    act = (gate * jax.nn.sigmoid(gate) * up).astype(BF16)
    acc_s[...] += _dot(act, wo_ref[...])

    @pl.when(j == pl.num_programs(2) - 1)
    def _():
        o_ref[0] = x_ref[0] + g2_ref[0] * acc_s[...]


def _ffn(x, g, sc, sh, g2, wi, wo):
    B, T, D = x.shape
    F = wo.shape[0]
    tm, th = 512, F // 2
    nh = F // th
    row = lambda b, i, j: (b, i, 0)
    per_b = lambda b, i, j: (b, 0, 0)
    return pl.pallas_call(
        _ffn_kernel,
        grid=(B, T // tm, nh),
        in_specs=[pl.BlockSpec((1, tm, D), row),
                  pl.BlockSpec((1, D), lambda b, i, j: (0, 0)),
                  pl.BlockSpec((1, 1, D), per_b),
                  pl.BlockSpec((1, 1, D), per_b),
                  pl.BlockSpec((1, 1, D), per_b),
                  pl.BlockSpec((D, th), lambda b, i, j: (0, j)),
                  pl.BlockSpec((D, th), lambda b, i, j: (0, j + nh)),
                  pl.BlockSpec((th, D), lambda b, i, j: (j, 0))],
        out_specs=pl.BlockSpec((1, tm, D), row),
        out_shape=jax.ShapeDtypeStruct((B, T, D), F32),
        scratch_shapes=[pltpu.VMEM((tm, D), BF16), pltpu.VMEM((tm, D), F32)],
        compiler_params=_cparams("parallel", "parallel", "arbitrary"),
        name="ffn_swiglu",
    )(x, g.reshape(1, D), sc.reshape(B, 1, D), sh.reshape(B, 1, D), g2.reshape(B, 1, D), wi, wi, wo)


def _rot_cols(w):
    half = HEAD_DIM // 2
    return jnp.concatenate([-w[..., half:], w[..., :half]], axis=-1)


def _rope_slot(w):
    return jnp.concatenate([w, _rot_cols(w)], axis=-1)


def _pad_slot(w):
    return jnp.pad(w, [(0, 0)] * (w.ndim - 1) + [(0, SLOT - w.shape[-1])])


def _perm_gain(g):
    half = HEAD_DIM // 2
    return jnp.concatenate([g, g[half:], g[:half]]).reshape(1, SLOT)


def _rope_table(pos):
    half = HEAD_DIM // 2
    inv = ROPE_THETA ** (-np.arange(half, dtype=np.float64) / half)
    ang = np.asarray(pos, np.float64)[:, None] * inv[None, :]
    cos, sin = np.cos(ang), np.sin(ang)
    return jnp.asarray(np.concatenate([cos, cos, sin, sin], axis=1), F32)


def _in_weights(w):
    D = w.shape[0]
    hd = HEAD_DIM
    o = 0
    qa = w[:, o:o + A_HEADS * hd]; o += A_HEADS * hd
    ka = w[:, o:o + hd]; o += hd
    va = w[:, o:o + hd]; o += hd
    iq = w[:, o:o + IDX_HEADS * hd]; o += IDX_HEADS * hd
    ik = w[:, o:o + hd]; o += hd
    iw = w[:, o:o + IDX_HEADS]; o += IDX_HEADS
    nb = 3 * B_WIDTH + DECAY_LORA + AAA_LORA + GATE_LORA
    wb = w[:, o:o + nb]; o += nb
    qc = w[:, o:o + C_HEADS * hd]; o += C_HEADS * hd
    kc, vc, ksl, vsl, kwn, vwn = (w[:, o + i * hd:o + (i + 1) * hd] for i in range(6)); o += 6 * hd
    gc = w[:, o:o + 3 * C_HEADS]
    heads = lambda m, n: [m[:, i * hd:(i + 1) * hd] for i in range(n)]
    wa = jnp.concatenate([_rope_slot(h) for h in heads(qa, A_HEADS)]
                         + [_rope_slot(h) for h in heads(iq, IDX_HEADS)]
                         + [_rope_slot(ka), _rope_slot(ik), _pad_slot(va), _pad_slot(iw)], axis=1)
    wc = jnp.concatenate([_rope_slot(h) for h in heads(qc, C_HEADS)]
                         + [_rope_slot(ksl), _rope_slot(kwn), jnp.concatenate([kc, vc], axis=1),
                            _pad_slot(vsl), _pad_slot(vwn), _pad_slot(gc)], axis=1)
    return wa.astype(BF16), wb.astype(BF16), wc.astype(BF16)


def _cmp_weights(pe, w1, w2):
    half = CMP_LEN // 2
    hd = HEAD_DIM
    zero = jnp.zeros((half, hd, CMP_HID), F32)

    def expand(lo):
        wk = w1[0].reshape(CMP_LEN, hd, CMP_HID)[lo:lo + half]
        wv = w1[1].reshape(CMP_LEN, hd, CMP_HID)[lo:lo + half]
        k_rows = jnp.concatenate([wk, zero], axis=1)
        v_rows = jnp.concatenate([zero, wv], axis=1)
        return jnp.concatenate([k_rows, v_rows], axis=2).reshape(half * SLOT, 2 * CMP_HID).astype(BF16)

    def pe_row(lo):
        return jnp.concatenate([pe[0, lo:lo + half], pe[1, lo:lo + half]], axis=1).reshape(1, half * SLOT)

    w2k = _rope_slot(w2[0]).astype(BF16)
    w2v = _pad_slot(w2[1]).astype(BF16)
    return pe_row(0), pe_row(half), expand(0), expand(half), w2k, w2v


def _nsa_tables(T):
    n_cmp = (T - CMP_LEN) // CMP_STRIDE + 1
    n_blk = T // SEL_LEN
    starts = np.arange(n_cmp) * CMP_STRIDE
    end_pos = starts + CMP_LEN - 1
    sel_start = np.arange(n_blk) * SEL_LEN
    ovl = np.zeros((SLOT, SLOT), np.float32)
    ovl[:n_cmp, :n_blk] = ((starts[:, None] <= sel_start[None, :] + SEL_LEN - 1)
                           & (end_pos[:, None] >= sel_start[None, :]))
    expand = np.zeros((SLOT, T), np.float32)
    expand[np.arange(T) // SEL_LEN, np.arange(T)] = 1.0
    m = T // CMP_STRIDE
    cse = _rope_table(np.arange(m) * CMP_STRIDE + CMP_LEN - 1)
    return n_cmp, jnp.asarray(ovl, BF16), jnp.asarray(expand, BF16), cse


def _out_weights(w):
    hd = HEAD_DIM
    D = w.shape[1]

    def slots(base, n):
        blocks = [jnp.concatenate([w[base + i * hd:base + (i + 1) * hd], jnp.zeros((SLOT - hd, D), F32)])
                  for i in range(n)]
        return jnp.concatenate(blocks).astype(BF16)

    a0 = 0
    b0 = A_HEADS * hd
    c0 = b0 + B_WIDTH
    return slots(a0, A_HEADS), w[b0:c0].reshape(B_HEADS, hd, D).astype(BF16), slots(c0, C_HEADS)


def kernel(x, c, ada_w, ada_b, norm1_g, w_in, dsa_q_g, dsa_k_g, rwkv_mu, rwkv_w0, rwkv_w2, rwkv_a0, rwkv_a2, rwkv_g2, rwkv_k_k, rwkv_k_a, rwkv_r_k, rwkv_ln_w, rwkv_ln_b, nsa_q_g, nsa_k_g, nsa_pe, nsa_w1, nsa_w2, w_out, norm2_g, ffn_wi, ffn_wo):
    B, T, D = x.shape
    L = w_in.shape[0]
    W = B_WIDTH
    assert T % 256 == 0 and T % RW_CHUNK == 0 and D % SLOT == 0
    mod = _modulation(c, ada_w, ada_b)
    cs = _rope_table(np.arange(T))
    n_cmp, ovl, expand, cse = _nsa_tables(T)
    bd = jnp.asarray(np.kron(np.eye(B_HEADS), np.ones((HEAD_DIM, HEAD_DIM))), BF16)
    lora_pad = jnp.zeros((SLOT - DECAY_LORA, W), F32)
    for l in range(L):
        sh1, sc1, g1, sh2, sc2, g2 = (mod[l, :, i * D:(i + 1) * D] for i in range(6))
        wa, wb, wc = _in_weights(w_in[l])
        za, zb, zc = _in_proj(x, norm1_g[l], sc1, sh1, wa, wb, wc)
        qa, iq, ka, ik, va, iw = _dsa_prep(za, cs, _perm_gain(dsa_q_g[l]), _perm_gain(dsa_k_g[l]))
        oa = _dsa(qa, iq, iw, ka, ik, va)
        w2e = jnp.concatenate([rwkv_w2[l], lora_pad]).astype(BF16)
        a2e = jnp.concatenate([lora_pad, rwkv_a2[l]]).astype(BF16)
        row = lambda p: p.reshape(1, W)
        r, k2, v, kk, bb, lw, gg, bon = _rw_pre(
            zb, rwkv_mu[l].reshape(1, -1), row(rwkv_w0[l]), w2e, row(rwkv_a0[l]), a2e,
            rwkv_g2[l].astype(BF16), row(rwkv_k_k[l]), row(rwkv_k_a[l]), row(rwkv_r_k[l]), bd)
        ry, yc, pm, qm = _rw_chunks(r, k2, v, kk, bb, lw)
        ob = _rw_scan(ry, yc, pm, qm, gg, bon, rwkv_ln_w[l].reshape(B_HEADS, 1, HEAD_DIM),
                      rwkv_ln_b[l].reshape(B_HEADS, 1, HEAD_DIM))
        gkc = _perm_gain(nsa_k_g[l])
        qc, ks, kw, vs, vw, gc = _nsa_prep(zc, cs, _perm_gain(nsa_q_g[l]), gkc)
        kv_slot = (C_HEADS + 2) * SLOT
        rm = zc[:, :, kv_slot:kv_slot + SLOT].reshape(B, T // CMP_STRIDE, CMP_STRIDE * SLOT)
        kc, vc = _nsa_compress(rm, *_cmp_weights(nsa_pe[l], nsa_w1[l], nsa_w2[l]), cse, gkc)
        oc = _nsa(qc, gc, kc, vc, ks, vs, kw, vw, ovl, expand, n_cmp)
        woa, wob, woc = _out_weights(w_out[l])
        x = _out_proj(x, oa, ob, oc, woa, wob, woc, g1)
        x = _ffn(x, norm2_g[l], sc2, sh2, g2, ffn_wi[l].astype(BF16), ffn_wo[l].astype(BF16))
    return x
```

```python
import functools

import numpy as np
import jax
import jax.numpy as jnp
from jax import lax
from jax.experimental import pallas as pl
from jax.experimental.pallas import tpu as pltpu

F32 = jnp.float32
BF16 = jnp.bfloat16

HEAD_DIM = 64
SLOT = 128
A_HEADS = 4
IDX_HEADS = 4
DSA_TOPK = 256
B_HEADS = 8
B_WIDTH = B_HEADS * HEAD_DIM
DECAY_LORA = 64
AAA_LORA = 64
GATE_LORA = 128
GN_EPS = 64e-5
C_HEADS = 4
CMP_LEN = 32
CMP_STRIDE = 16
CMP_HID = 256
SEL_LEN = 64
SEL_TOPN = 16
WIN = 512
Q_BLOCK = 128
ROPE_THETA = 10000.0
NORM_EPS = 1e-6
NEG = -1e30
INT_MIN = -2147483648
RW_CHUNK = 64
RW_SCAN_CHUNKS = 4
PREFIX_GROUPS = 8
VMEM_LIMIT_BYTES = 56 * 1024 * 1024


def _cparams(*sem):
    return pltpu.CompilerParams(dimension_semantics=sem, vmem_limit_bytes=VMEM_LIMIT_BYTES)


def _dot(a, b):
    return jnp.dot(a, b, preferred_element_type=F32)


def _dot_nt(a, b):
    return lax.dot_general(a, b, (((1,), (1,)), ((), ())), preferred_element_type=F32)


def _split3(x):
    hi = x.astype(BF16)
    r1 = x - hi.astype(F32)
    mid = r1.astype(BF16)
    lo = (r1 - mid.astype(F32)).astype(BF16)
    return hi, mid, lo


def _dot3(x, w):
    hi, mid, lo = _split3(x)
    return _dot(hi, w) + _dot(mid, w) + _dot(lo, w)


def _bmm(a, b):
    return jnp.einsum('hts,hsd->htd', a, b, preferred_element_type=F32)


def _bmm_nt(a, b):
    return jnp.einsum('htj,hsj->hts', a, b, preferred_element_type=F32)


def _bmm_tn(a, b):
    return jnp.einsum('htj,htd->hjd', a, b, preferred_element_type=F32)


def _mod_kernel(c_ref, w_ref, b_ref, o_ref):
    c = c_ref[...]
    s = (c * jax.nn.sigmoid(c)).astype(BF16)
    o_ref[0] = _dot(s, w_ref[0]) + b_ref[0]


def _modulation(c, ada_w, ada_b):
    L, D, N = ada_w.shape
    B = c.shape[0]
    tn = 1536
    return pl.pallas_call(
        _mod_kernel,
        grid=(L, N // tn),
        in_specs=[pl.BlockSpec((B, D), lambda l, j: (0, 0)),
                  pl.BlockSpec((1, D, tn), lambda l, j: (l, 0, j)),
                  pl.BlockSpec((1, 1, tn), lambda l, j: (l, 0, j))],
        out_specs=pl.BlockSpec((1, B, tn), lambda l, j: (l, 0, j)),
        out_shape=jax.ShapeDtypeStruct((L, B, N), F32),
        compiler_params=_cparams("parallel", "parallel"),
        name="adaln_modulation",
    )(c, ada_w.astype(BF16), ada_b.reshape(L, 1, N))


def _in_kernel(x_ref, g_ref, sc_ref, sh_ref, wa_ref, wb_ref, wc_ref, za_ref, zb_ref, zc_ref):
    x = x_ref[0]
    y = x * lax.rsqrt(jnp.mean(x * x, axis=-1, keepdims=True) + NORM_EPS) * g_ref[...]
    h = (y * (1.0 + sc_ref[0]) + sh_ref[0]).astype(BF16)
    za_ref[0] = _dot(h, wa_ref[...])
    zb_ref[0] = _dot(h, wb_ref[...])
    zc_ref[0] = _dot(h, wc_ref[...])


def _in_proj(x, g, sc, sh, wa, wb, wc):
    B, T, D = x.shape
    tm = 256
    na, nb, nc = wa.shape[1], wb.shape[1], wc.shape[1]
    row = lambda b, i: (b, i, 0)
    fix = lambda b, i: (0, 0)
    per_b = lambda b, i: (b, 0, 0)
    return pl.pallas_call(
        _in_kernel,
        grid=(B, T // tm),
        in_specs=[pl.BlockSpec((1, tm, D), row),
                  pl.BlockSpec((1, D), fix),
                  pl.BlockSpec((1, 1, D), per_b),
                  pl.BlockSpec((1, 1, D), per_b),
                  pl.BlockSpec((D, na), fix),
                  pl.BlockSpec((D, nb), fix),
                  pl.BlockSpec((D, nc), fix)],
        out_specs=[pl.BlockSpec((1, tm, na), row),
                   pl.BlockSpec((1, tm, nb), row),
                   pl.BlockSpec((1, tm, nc), row)],
        out_shape=[jax.ShapeDtypeStruct((B, T, na), F32),
                   jax.ShapeDtypeStruct((B, T, nb), F32),
                   jax.ShapeDtypeStruct((B, T, nc), F32)],
        compiler_params=_cparams("parallel", "parallel"),
        name="in_proj",
    )(x, g.reshape(1, D), sc.reshape(B, 1, D), sh.reshape(B, 1, D), wa, wb, wc)


def _norm_rope(x, table):
    rs = lax.rsqrt(jnp.mean(x * x, axis=-1, keepdims=True) + NORM_EPS)
    return x * rs * table


def _fold(y):
    return y + pltpu.roll(y, HEAD_DIM, 1)


def _dsa_prep_kernel(za_ref, cs_ref, gq_ref, gk_ref, qa_ref, iq_ref, ka_ref, ik_ref, va_ref, iw_ref):
    cs = cs_ref[...]
    tq = cs * gq_ref[...] * (HEAD_DIM ** -0.5)
    tk = cs * gk_ref[...]
    for h in range(A_HEADS):
        sl = slice(h * SLOT, (h + 1) * SLOT)
        qa_ref[0, :, sl] = _norm_rope(za_ref[0, :, sl], tq).astype(BF16)
    for h in range(IDX_HEADS):
        sl = slice(h * SLOT, (h + 1) * SLOT)
        src = slice((A_HEADS + h) * SLOT, (A_HEADS + h + 1) * SLOT)
        iq_ref[0, :, sl] = (za_ref[0, :, src] * cs).astype(BF16)
    base = (A_HEADS + IDX_HEADS) * SLOT
    ka_ref[0] = _fold(_norm_rope(za_ref[0, :, base:base + SLOT], tk)).astype(BF16)
    ik_ref[0] = _fold(za_ref[0, :, base + SLOT:base + 2 * SLOT] * cs).astype(BF16)
    va_ref[0] = za_ref[0, :, base + 2 * SLOT:base + 3 * SLOT].astype(BF16)
    iw_ref[0] = za_ref[0, :, base + 3 * SLOT:base + 4 * SLOT]


def _dsa_prep(za, cs, gq, gk):
    B, T, NA = za.shape
    tm = 256
    row = lambda b, i: (b, i, 0)
    fix = lambda b, i: (0, 0)
    outs = [(A_HEADS * SLOT, BF16), (IDX_HEADS * SLOT, BF16), (SLOT, BF16), (SLOT, BF16), (SLOT, BF16),
            (SLOT, F32)]
    return pl.pallas_call(
        _dsa_prep_kernel,
        grid=(B, T // tm),
        in_specs=[pl.BlockSpec((1, tm, NA), row),
                  pl.BlockSpec((tm, SLOT), lambda b, i: (i, 0)),
                  pl.BlockSpec((1, SLOT), fix),
                  pl.BlockSpec((1, SLOT), fix)],
        out_specs=[pl.BlockSpec((1, tm, w), row) for w, _ in outs],
        out_shape=[jax.ShapeDtypeStruct((B, T, w), dt) for w, dt in outs],
        compiler_params=_cparams("parallel", "parallel"),
        name="dsa_prep",
    )(za, cs, gq, gk)


def _masked_attend(q_ref, k, v, mask, heads):
    for h in range(heads):
        sl = slice(h * SLOT, (h + 1) * SLOT)
        s = jnp.where(mask, _dot_nt(q_ref[0, :, sl], k), NEG)
        m = jnp.max(s, axis=1, keepdims=True)
        p = jnp.exp(s - m)
        l = jnp.sum(p, axis=1, keepdims=True)
        yield h, _dot(p.astype(BF16), v) / l


def _dsa_block(qa_ref, iq_ref, iw_ref, ka_ref, ik_ref, va_ref, o_ref, t0, Te, topk):
    rows = t0 + lax.broadcasted_iota(jnp.int32, (Q_BLOCK, 1), 0)
    cols = lax.broadcasted_iota(jnp.int32, (1, Te), 1)
    causal = cols <= rows
    if Te <= topk:
        mask = causal
    else:
        ik = ik_ref[0, :Te, :]
        iw = iw_ref[0]
        score = jnp.zeros((Q_BLOCK, Te), F32)
        for h in range(IDX_HEADS):
            s = _dot_nt(iq_ref[0, :, h * SLOT:(h + 1) * SLOT], ik)
            score = score + iw[:, h:h + 1] * jnp.maximum(s, 0.0)
        score = jnp.where(score == 0.0, 0.0, score)
        bits = lax.bitcast_convert_type(score, jnp.int32)
        key = bits ^ ((bits >> 31) & jnp.int32(0x7FFFFFFF))
        key = jnp.where(causal, key, jnp.int32(INT_MIN))
        kf = float(topk)

        def count_ge(cand):
            return jnp.sum(jnp.where(key >= cand, 1.0, 0.0), axis=1, keepdims=True)

        thr0 = jnp.where(count_ge(jnp.int32(0)) >= kf, jnp.int32(0), jnp.int32(INT_MIN))

        def body(i, thr):
            cand = thr | jnp.left_shift(jnp.int32(1), 30 - i)
            return jnp.where(count_ge(cand) >= kf, cand, thr)

        thr = lax.fori_loop(0, 31, body, thr0)
        gt = key > thr
        eq = key == thr
        need = kf - jnp.sum(jnp.where(gt, 1.0, 0.0), axis=1, keepdims=True)
        ri = lax.broadcasted_iota(jnp.int32, (SLOT, SLOT), 0)
        ci = lax.broadcasted_iota(jnp.int32, (SLOT, SLOT), 1)
        upper = jnp.where(ri <= ci, 1.0, 0.0).astype(BF16)
        run = jnp.zeros((Q_BLOCK, 1), F32)
        sel = []
        for c in range(Te // SLOT):
            sl = slice(c * SLOT, (c + 1) * SLOT)
            e = jnp.where(eq[:, sl], 1.0, 0.0)
            incl = _dot(e.astype(BF16), upper)
            sel.append(gt[:, sl] | (eq[:, sl] & (incl - e + run < need)))
            run = run + incl[:, SLOT - 1:SLOT]
        mask = jnp.concatenate(sel, axis=1) & causal
    for h, o in _masked_attend(qa_ref, ka_ref[0, :Te, :], va_ref[0, :Te, :], mask, A_HEADS):
        o_ref[0, :, h * SLOT:(h + 1) * SLOT] = o


def _dsa_kernel(qa_ref, iq_ref, iw_ref, ka_ref, ik_ref, va_ref, o_ref, *, T, topk, groups):
    qi = pl.program_id(1)
    per = (T // Q_BLOCK) // groups
    for g in range(groups):
        @pl.when(qi // per == g)
        def _(g=g):
            _dsa_block(qa_ref, iq_ref, iw_ref, ka_ref, ik_ref, va_ref, o_ref, qi * Q_BLOCK,
                       (g + 1) * per * Q_BLOCK, topk)


def _dsa(qa, iq, iw, ka, ik, va):
    B, T, _ = qa.shape
    topk = min(DSA_TOPK, T // 4)
    blk = lambda b, i: (b, i, 0)
    full = lambda b, i: (b, 0, 0)
    return pl.pallas_call(
        functools.partial(_dsa_kernel, T=T, topk=topk, groups=PREFIX_GROUPS),
        grid=(B, T // Q_BLOCK),
        in_specs=[pl.BlockSpec((1, Q_BLOCK, A_HEADS * SLOT), blk),
                  pl.BlockSpec((1, Q_BLOCK, IDX_HEADS * SLOT), blk),
                  pl.BlockSpec((1, Q_BLOCK, SLOT), blk),
                  pl.BlockSpec((1, T, SLOT), full),
                  pl.BlockSpec((1, T, SLOT), full),
                  pl.BlockSpec((1, T, SLOT), full)],
        out_specs=pl.BlockSpec((1, Q_BLOCK, A_HEADS * SLOT), blk),
        out_shape=jax.ShapeDtypeStruct((B, T, A_HEADS * SLOT), F32),
        compiler_params=_cparams("parallel", "parallel"),
        name="dsa_attention",
    )(qa, iq, iw, ka, ik, va)


def _nsa_prep_kernel(zc_ref, cs_ref, gq_ref, gk_ref, qc_ref, ks_ref, kw_ref, vs_ref, vw_ref, gc_ref):
    cs = cs_ref[...]
    tq = cs * gq_ref[...] * (HEAD_DIM ** -0.5)
    tk = cs * gk_ref[...]
    for h in range(C_HEADS):
        sl = slice(h * SLOT, (h + 1) * SLOT)
        qc_ref[0, :, sl] = _norm_rope(zc_ref[0, :, sl], tq).astype(BF16)
    base = C_HEADS * SLOT
    ks_ref[0] = _fold(_norm_rope(zc_ref[0, :, base:base + SLOT], tk)).astype(BF16)
    kw_ref[0] = _fold(_norm_rope(zc_ref[0, :, base + SLOT:base + 2 * SLOT], tk)).astype(BF16)
    vs_ref[0] = zc_ref[0, :, base + 3 * SLOT:base + 4 * SLOT].astype(BF16)
    vw_ref[0] = zc_ref[0, :, base + 4 * SLOT:base + 5 * SLOT].astype(BF16)
    gc_ref[0] = zc_ref[0, :, base + 5 * SLOT:base + 6 * SLOT]


def _nsa_prep(zc, cs, gq, gk):
    B, T, NC = zc.shape
    tm = 256
    row = lambda b, i: (b, i, 0)
    fix = lambda b, i: (0, 0)
    outs = [(C_HEADS * SLOT, BF16), (SLOT, BF16), (SLOT, BF16), (SLOT, BF16), (SLOT, BF16), (SLOT, F32)]
    return pl.pallas_call(
        _nsa_prep_kernel,
        grid=(B, T // tm),
        in_specs=[pl.BlockSpec((1, tm, NC), row),
                  pl.BlockSpec((tm, SLOT), lambda b, i: (i, 0)),
                  pl.BlockSpec((1, SLOT), fix),
                  pl.BlockSpec((1, SLOT), fix)],
        out_specs=[pl.BlockSpec((1, tm, w), row) for w, _ in outs],
        out_shape=[jax.ShapeDtypeStruct((B, T, w), dt) for w, dt in outs],
        compiler_params=_cparams("parallel", "parallel"),
        name="nsa_prep",
    )(zc, cs, gq, gk)


def _nsa_cmp_kernel(rm_ref, pet_ref, peb_ref, w1t_ref, w1b_ref, w2k_ref, w2v_ref, cse_ref, gk_ref,
                    kc_ref, vc_ref):
    rm = rm_ref[0]
    p = _dot((rm + pet_ref[...]).astype(BF16), w1t_ref[...])
    q = _dot((rm + peb_ref[...]).astype(BF16), w1b_ref[...])
    nrow = rm.shape[0]
    hid = jax.nn.gelu(p + pltpu.roll(q, nrow - 1, 0), approximate=True)
    kc = _dot(hid[:, :CMP_HID].astype(BF16), w2k_ref[...])
    kc_ref[0] = _fold(_norm_rope(kc, cse_ref[...] * gk_ref[...])).astype(BF16)
    vc_ref[0] = _dot(hid[:, CMP_HID:].astype(BF16), w2v_ref[...]).astype(BF16)


def _nsa_compress(rm, pet, peb, w1t, w1b, w2k, w2v, cse, gk):
    B, M, W = rm.shape
    fix = lambda b: (0, 0)
    return pl.pallas_call(
        _nsa_cmp_kernel,
        grid=(B,),
        in_specs=[pl.BlockSpec((1, M, W), lambda b: (b, 0, 0)),
                  pl.BlockSpec((1, W), fix), pl.BlockSpec((1, W), fix),
                  pl.BlockSpec(w1t.shape, fix), pl.BlockSpec(w1b.shape, fix),
                  pl.BlockSpec(w2k.shape, fix), pl.BlockSpec(w2v.shape, fix),
                  pl.BlockSpec(cse.shape, fix), pl.BlockSpec((1, SLOT), fix)],
        out_specs=[pl.BlockSpec((1, M, SLOT), lambda b: (b, 0, 0)),
                   pl.BlockSpec((1, M, SLOT), lambda b: (b, 0, 0))],
        out_shape=[jax.ShapeDtypeStruct((B, M, SLOT), BF16), jax.ShapeDtypeStruct((B, M, SLOT), BF16)],
        compiler_params=_cparams("parallel"),
        name="nsa_compress",
    )(rm, pet, peb, w1t, w1b, w2k, w2v, cse, gk)


def _nsa_kernel(qc_ref, gc_ref, kc_ref, vc_ref, ks_ref, vs_ref, kw_ref, vw_ref, ovl_ref, exp_ref, o_ref,
                slc_s, *, T, n_cmp, n_blk, top_n, win_len, groups):
    t0 = pl.program_id(1) * Q_BLOCK
    rows = t0 + lax.broadcasted_iota(jnp.int32, (Q_BLOCK, 1), 0)
    lane = lax.broadcasted_iota(jnp.int32, (1, SLOT), 1)
    is_cmp = lane < n_cmp
    cvalid = is_cmp & (lane * CMP_STRIDE + (CMP_LEN - 1) <= rows)
    kc = kc_ref[0]
    vc = vc_ref[0]
    psum = jnp.zeros((Q_BLOCK, SLOT), F32)
    o_cmp = []
    for h in range(C_HEADS):
        s = _dot_nt(qc_ref[0, :, h * SLOT:(h + 1) * SLOT], kc)
        s = jnp.where(is_cmp, jnp.where(cvalid, s, NEG), -jnp.inf)
        e = jnp.exp(s - jnp.max(s, axis=1, keepdims=True))
        p = jnp.where(cvalid, e / jnp.sum(e, axis=1, keepdims=True), 0.0)
        psum = psum + p
        o_cmp.append(_dot(p.astype(BF16), vc))
    imp = _dot3(psum, ovl_ref[...])
    cur = rows >> 6
    adm = lane * SEL_LEN <= rows
    forced = (lane == 0) | (lane == cur) | (lane == cur - 1)
    val = jnp.where(adm, jnp.where(forced, jnp.inf, imp), -jnp.inf)
    val = jnp.where(lane < n_blk, val, -jnp.inf)
    rank = jnp.zeros((Q_BLOCK, SLOT), F32)
    for j in range(n_blk):
        c = val[:, j:j + 1]
        rank = rank + jnp.where((c > val) | ((c == val) & (lane > j)), 1.0, 0.0)
    sel = jnp.where((rank < float(top_n)) & (lane < n_blk), 1.0, 0.0).astype(BF16)
    qi = pl.program_id(1)
    per = (T // Q_BLOCK) // groups
    for gi in range(groups):
        @pl.when(qi // per == gi)
        def _(gi=gi):
            te = (gi + 1) * per * Q_BLOCK
            cols = lax.broadcasted_iota(jnp.int32, (1, te), 1)
            mask = (_dot(sel, exp_ref[:, :te]) > 0.5) & (cols <= rows)
            for h, o in _masked_attend(qc_ref, ks_ref[0, :te, :], vs_ref[0, :te, :], mask, C_HEADS):
                slc_s[:, h * SLOT:(h + 1) * SLOT] = o
    ws = pl.multiple_of(jnp.clip(t0 - WIN, 0, T - win_len), Q_BLOCK)
    wpos = ws + lax.broadcasted_iota(jnp.int32, (1, win_len), 1)
    wmask = (wpos <= rows) & (wpos > rows - WIN)
    kw = kw_ref[0, pl.ds(ws, win_len), :]
    vw = vw_ref[0, pl.ds(ws, win_len), :]
    o_win = [o for _, o in _masked_attend(qc_ref, kw, vw, wmask, C_HEADS)]
    g = jax.nn.sigmoid(gc_ref[0])
    for h in range(C_HEADS):
        o_ref[0, :, h * SLOT:(h + 1) * SLOT] = (g[:, 3 * h:3 * h + 1] * o_cmp[h]
                                                + g[:, 3 * h + 1:3 * h + 2] * slc_s[:, h * SLOT:(h + 1) * SLOT]
                                                + g[:, 3 * h + 2:3 * h + 3] * o_win[h])


def _nsa(qc, gc, kc, vc, ks, vs, kw, vw, ovl, expand, n_cmp):
    B, T, _ = qc.shape
    n_blk = T // SEL_LEN
    blk = lambda b, i: (b, i, 0)
    full = lambda b, i: (b, 0, 0)
    fix = lambda b, i: (0, 0)
    M = kc.shape[1]
    kern = functools.partial(_nsa_kernel, T=T, n_cmp=n_cmp, n_blk=n_blk, top_n=min(SEL_TOPN, n_blk),
                             win_len=min(WIN + Q_BLOCK, T), groups=PREFIX_GROUPS)
    return pl.pallas_call(
        kern,
        grid=(B, T // Q_BLOCK),
        in_specs=[pl.BlockSpec((1, Q_BLOCK, C_HEADS * SLOT), blk),
                  pl.BlockSpec((1, Q_BLOCK, SLOT), blk),
                  pl.BlockSpec((1, M, SLOT), full),
                  pl.BlockSpec((1, M, SLOT), full),
                  pl.BlockSpec((1, T, SLOT), full),
                  pl.BlockSpec((1, T, SLOT), full),
                  pl.BlockSpec((1, T, SLOT), full),
                  pl.BlockSpec((1, T, SLOT), full),
                  pl.BlockSpec(ovl.shape, fix),
                  pl.BlockSpec(expand.shape, fix)],
        out_specs=pl.BlockSpec((1, Q_BLOCK, C_HEADS * SLOT), blk),
        out_shape=jax.ShapeDtypeStruct((B, T, C_HEADS * SLOT), F32),
        scratch_shapes=[pltpu.VMEM((Q_BLOCK, C_HEADS * SLOT), F32)],
        compiler_params=_cparams("parallel", "parallel"),
        name="nsa_attention",
    )(qc, gc, kc, vc, ks, vs, kw, vw, ovl, expand)


def _rw_pre_kernel(zb_ref, mu_ref, w0_ref, w2_ref, a0_ref, a2_ref, g2_ref, kk_ref, ka_ref, rk_ref, bd_ref,
                   r_o, k_o, v_o, kk_o, b_o, lw_o, g_o, bon_o, carry):
    @pl.when(pl.program_id(1) == 0)
    def _():
        carry[...] = jnp.zeros_like(carry)

    zt = zb_ref[0]
    tm = zt.shape[0]
    row0 = lax.broadcasted_iota(jnp.int32, (tm, 1), 0) == 0
    prev = jnp.where(row0, carry[7:8, :], pltpu.roll(zt, 1, 0))
    carry[...] = zt[tm - 8:tm, :]
    z = zt + (prev - zt) * mu_ref[...]
    W = B_WIDTH
    r, k, v = z[:, :W], z[:, W:2 * W], z[:, 2 * W:3 * W]
    lora = z[:, 3 * W:3 * W + SLOT]
    gd = z[:, 3 * W + SLOT:3 * W + SLOT + GATE_LORA]
    wl = w0_ref[...] + _dot(jnp.tanh(lora).astype(BF16), w2_ref[...])
    w_log = -jax.nn.softplus(-wl) - 0.5
    lw = -jnp.exp(w_log)
    a = jax.nn.sigmoid(a0_ref[...] + _dot(lora.astype(BF16), a2_ref[...]))
    g = _dot(jax.nn.sigmoid(gd).astype(BF16), g2_ref[...])
    kk = k * kk_ref[...]
    bd = bd_ref[...]
    kk = kk / jnp.maximum(jnp.sqrt(_dot3(kk * kk, bd)), 1e-12)
    k2 = k * (1.0 + (a - 1.0) * ka_ref[...])
    bon = _dot3(r * k2 * rk_ref[...], bd) * v
    outs = ((r_o, r), (k_o, k2), (v_o, v), (kk_o, kk), (b_o, kk * a), (lw_o, lw), (g_o, g), (bon_o, bon))
    for ref, val in outs:
        for h in range(B_HEADS):
            ref[0, h] = val[:, h * HEAD_DIM:(h + 1) * HEAD_DIM]


def _rw_pre(zb, mu, w0, w2e, a0, a2e, g2, k_k, k_a, r_k, bd):
    B, T, NB = zb.shape
    tm = 256
    W = B_WIDTH
    fix = lambda b, i: (0, 0)
    hm = pl.BlockSpec((1, B_HEADS, tm, HEAD_DIM), lambda b, i: (b, 0, i, 0))
    return pl.pallas_call(
        _rw_pre_kernel,
        grid=(B, T // tm),
        in_specs=[pl.BlockSpec((1, tm, NB), lambda b, i: (b, i, 0)),
                  pl.BlockSpec((1, NB), fix),
                  pl.BlockSpec((1, W), fix), pl.BlockSpec((SLOT, W), fix),
                  pl.BlockSpec((1, W), fix), pl.BlockSpec((SLOT, W), fix),
                  pl.BlockSpec((GATE_LORA, W), fix),
                  pl.BlockSpec((1, W), fix), pl.BlockSpec((1, W), fix), pl.BlockSpec((1, W), fix),
                  pl.BlockSpec((W, W), fix)],
        out_specs=[hm] * 8,
        out_shape=[jax.ShapeDtypeStruct((B, B_HEADS, T, HEAD_DIM), F32)] * 8,
        scratch_shapes=[pltpu.VMEM((8, NB), F32)],
        compiler_params=_cparams("parallel", "arbitrary"),
        name="rwkv_prep",
    )(zb, mu, w0, w2e, a0, a2e, g2, k_k, k_a, r_k, bd)


def _rw_chunk_kernel(r_ref, k_ref, v_ref, kk_ref, b_ref, lw_ref, ry_o, yc_o, pm_o, qm_o):
    r, k, v, kk, b, lw = (x[0] for x in (r_ref, k_ref, v_ref, kk_ref, b_ref, lw_ref))
    H, C, N = r.shape
    ti = lax.broadcasted_iota(jnp.int32, (H, C, C), 1)
    si = lax.broadcasted_iota(jnp.int32, (H, C, C), 2)
    tri = jnp.where(si <= ti, 1.0, 0.0).astype(BF16)
    h1, h2, _ = _split3(lw)
    cum = _bmm(tri, h1) + _bmm(tri, h2)
    tot = cum[:, C - 1:C, :]
    e_in = jnp.exp(cum)
    e_out = jnp.exp(-cum)
    a_s = (kk * jnp.exp(cum - lw)).astype(BF16)
    r_s = r * e_in
    r_b = r_s.astype(BF16)
    b_s = (b * e_out).astype(BF16)
    k_s = (k * e_out).astype(BF16)
    e_end = jnp.exp(tot - cum)
    b_e = (b * e_end).astype(BF16)
    k_e = (k * e_end).astype(BF16)
    vb = v.astype(BF16)
    low = si < ti
    lowi = si <= ti
    l_ab = jnp.where(low, _bmm_nt(a_s, b_s), 0.0)
    l_ak = jnp.where(low, _bmm_nt(a_s, k_s), 0.0).astype(BF16)
    m_rb = jnp.where(lowi, _bmm_nt(r_b, b_s), 0.0).astype(BF16)
    m_rk = jnp.where(lowi, _bmm_nt(r_b, k_s), 0.0).astype(BF16)
    inv = jnp.where(si == ti, 1.0, 0.0) - jnp.where(((ti % 2) == 1) & (si == ti - 1), l_ab, 0.0)
    m = 2
    while m < C:
        pair = (((ti // m) % 2) == 1) & ((si // m) == (ti // m) - 1)
        lm = jnp.where(pair, l_ab, 0.0).astype(BF16)
        ib = inv.astype(BF16)
        inv = inv - _bmm(_bmm(ib, lm).astype(BF16), ib)
        m *= 2
    ib = inv.astype(BF16)
    w1 = _bmm(ib, a_s).astype(BF16)
    w2 = _bmm(ib, _bmm(l_ak, vb).astype(BF16)).astype(BF16)
    ry_o[0] = r_s - _bmm(m_rb, w1)
    yc_o[0] = _bmm(m_rk, vb) - _bmm(m_rb, w2)
    ji = lax.broadcasted_iota(jnp.int32, (H, N, N), 1)
    di = lax.broadcasted_iota(jnp.int32, (H, N, N), 2)
    decay = jnp.where(ji == di, jnp.broadcast_to(jnp.exp(tot), (H, N, N)), 0.0)
    pm_o[0] = decay - _bmm_tn(b_e, w1)
    qm_o[0] = _bmm_tn(k_e, vb) - _bmm_tn(b_e, w2)


def _rw_chunks(r, k, v, kk, b, lw):
    B, H, T, N = r.shape
    C = RW_CHUNK
    nc = T // C
    tok = pl.BlockSpec((1, H, C, N), lambda bi, c: (bi, 0, c, 0))
    return pl.pallas_call(
        _rw_chunk_kernel,
        grid=(B, nc),
        in_specs=[tok] * 6,
        out_specs=[tok, tok, tok, tok],
        out_shape=[jax.ShapeDtypeStruct((B, H, T, N), F32)] * 2
        + [jax.ShapeDtypeStruct((B, H, nc * N, N), F32)] * 2,
        compiler_params=_cparams("parallel", "parallel"),
        name="rwkv_chunk_ops",
    )(r, k, v, kk, b, lw)


def _rw_scan_kernel(ry_ref, yc_ref, pm_ref, qm_ref, g_ref, bon_ref, lnw_ref, lnb_ref, o_ref, state):
    @pl.when(pl.program_id(1) == 0)
    def _():
        state[...] = jnp.zeros_like(state)

    C = N = RW_CHUNK
    g0 = state[...]
    for i in range(ry_ref.shape[2] // C):
        tok = slice(i * C, (i + 1) * C)
        mat = slice(i * N, (i + 1) * N)
        gh = g0.astype(BF16)
        gl = (g0 - gh.astype(F32)).astype(BF16)
        ry = ry_ref[0, :, tok, :].astype(BF16)
        pm = pm_ref[0, :, mat, :].astype(BF16)
        y = _bmm(ry, gh) + _bmm(ry, gl) + yc_ref[0, :, tok, :]
        g0 = _bmm(pm, gh) + _bmm(pm, gl) + qm_ref[0, :, mat, :]
        mean = jnp.mean(y, axis=-1, keepdims=True)
        d = y - mean
        var = jnp.mean(d * d, axis=-1, keepdims=True)
        yn = d * lax.rsqrt(var + GN_EPS) * lnw_ref[...] + lnb_ref[...]
        o_ref[0, :, tok, :] = (yn + bon_ref[0, :, tok, :]) * g_ref[0, :, tok, :]
    state[...] = g0


def _rw_scan(ry, yc, pm, qm, g, bon, lnw, lnb):
    B, H, T, N = ry.shape
    C = RW_SCAN_CHUNKS * RW_CHUNK
    tok = pl.BlockSpec((1, H, C, N), lambda bi, c: (bi, 0, c, 0))
    mat = pl.BlockSpec((1, H, RW_SCAN_CHUNKS * N, N), lambda bi, c: (bi, 0, c, 0))
    par = pl.BlockSpec((H, 1, N), lambda bi, c: (0, 0, 0))
    return pl.pallas_call(
        _rw_scan_kernel,
        grid=(B, T // C),
        in_specs=[tok, tok, mat, mat, tok, tok, par, par],
        out_specs=tok,
        out_shape=jax.ShapeDtypeStruct((B, H, T, N), F32),
        scratch_shapes=[pltpu.VMEM((H, N, N), F32)],
        compiler_params=_cparams("parallel", "arbitrary"),
        name="rwkv_scan",
    )(ry, yc, pm, qm, g, bon, lnw, lnb)


def _out_kernel(x_ref, oa_ref, ob_ref, oc_ref, woa_ref, wob_ref, woc_ref, g1_ref, o_ref):
    acc = _dot(oa_ref[0].astype(BF16), woa_ref[...]) + _dot(oc_ref[0].astype(BF16), woc_ref[...])
    for h in range(B_HEADS):
        acc = acc + _dot(ob_ref[0, h].astype(BF16), wob_ref[h])
    o_ref[0] = x_ref[0] + g1_ref[0] * acc


def _out_proj(x, oa, ob, oc, woa, wob, woc, g1):
    B, T, D = x.shape
    tm = 512
    row = lambda b, i: (b, i, 0)
    fix2 = lambda b, i: (0, 0)
    return pl.pallas_call(
        _out_kernel,
        grid=(B, T // tm),
        in_specs=[pl.BlockSpec((1, tm, D), row),
                  pl.BlockSpec((1, tm, oa.shape[2]), row),
                  pl.BlockSpec((1, B_HEADS, tm, HEAD_DIM), lambda b, i: (b, 0, i, 0)),
                  pl.BlockSpec((1, tm, oc.shape[2]), row),
                  pl.BlockSpec(woa.shape, fix2),
                  pl.BlockSpec(wob.shape, lambda b, i: (0, 0, 0)),
                  pl.BlockSpec(woc.shape, fix2),
                  pl.BlockSpec((1, 1, D), lambda b, i: (b, 0, 0))],
        out_specs=pl.BlockSpec((1, tm, D), row),
        out_shape=jax.ShapeDtypeStruct((B, T, D), F32),
        compiler_params=_cparams("parallel", "parallel"),
        name="out_proj",
    )(x, oa, ob, oc, woa, wob, woc, g1.reshape(B, 1, D))


def _ffn_kernel(x_ref, g_ref, sc_ref, sh_ref, g2_ref, wg_ref, wu_ref, wo_ref, o_ref, h_s, acc_s):
    j = pl.program_id(2)

    @pl.when(j == 0)
    def _():
        x = x_ref[0]
        y = x * lax.rsqrt(jnp.mean(x * x, axis=-1, keepdims=True) + NORM_EPS) * g_ref[...]
        h_s[...] = (y * (1.0 + sc_ref[0]) + sh_ref[0]).astype(BF16)
        acc_s[...] = jnp.zeros_like(acc_s)

    h = h_s[...]
    gate = _dot(h, wg_ref[...])
    up = _dot(h, wu_ref[...])
    act = (gate * jax.nn.sigmoid(gate) * up).astype(BF16)
    acc_s[...] += _dot(act, wo_ref[...])

    @pl.when(j == pl.num_programs(2) - 1)
    def _():
        o_ref[0] = x_ref[0] + g2_ref[0] * acc_s[...]


def _ffn(x, g, sc, sh, g2, wi, wo):
    B, T, D = x.shape
    F = wo.shape[0]
    tm, th = 512, F // 2
    nh = F // th
    row = lambda b, i, j: (b, i, 0)
    per_b = lambda b, i, j: (b, 0, 0)
    return pl.pallas_call(
        _ffn_kernel,
        grid=(B, T // tm, nh),
        in_specs=[pl.BlockSpec((1, tm, D), row),
                  pl.BlockSpec((1, D), lambda b, i, j: (0, 0)),
                  pl.BlockSpec((1, 1, D), per_b),
                  pl.BlockSpec((1, 1, D), per_b),
                  pl.BlockSpec((1, 1, D), per_b),
                  pl.BlockSpec((D, th), lambda b, i, j: (0, j)),
                  pl.BlockSpec((D, th), lambda b, i, j: (0, j + nh)),
                  pl.BlockSpec((th, D), lambda b, i, j: (j, 0))],
        out_specs=pl.BlockSpec((1, tm, D), row),
        out_shape=jax.ShapeDtypeStruct((B, T, D), F32),
        scratch_shapes=[pltpu.VMEM((tm, D), BF16), pltpu.VMEM((tm, D), F32)],
        compiler_params=_cparams("parallel", "parallel", "arbitrary"),
        name="ffn_swiglu",
    )(x, g.reshape(1, D), sc.reshape(B, 1, D), sh.reshape(B, 1, D), g2.reshape(B, 1, D), wi, wi, wo)


def _rot_cols(w):
    half = HEAD_DIM // 2
    return jnp.concatenate([-w[..., half:], w[..., :half]], axis=-1)


def _rope_slot(w):
    return jnp.concatenate([w, _rot_cols(w)], axis=-1)


def _pad_slot(w):
    return jnp.pad(w, [(0, 0)] * (w.ndim - 1) + [(0, SLOT - w.shape[-1])])


def _perm_gain(g):
    half = HEAD_DIM // 2
    return jnp.concatenate([g, g[half:], g[:half]]).reshape(1, SLOT)


def _rope_table(pos):
    half = HEAD_DIM // 2
    inv = ROPE_THETA ** (-np.arange(half, dtype=np.float64) / half)
    ang = np.asarray(pos, np.float64)[:, None] * inv[None, :]
    cos, sin = np.cos(ang), np.sin(ang)
    return jnp.asarray(np.concatenate([cos, cos, sin, sin], axis=1), F32)


def _in_weights(w):
    D = w.shape[0]
    hd = HEAD_DIM
    o = 0
    qa = w[:, o:o + A_HEADS * hd]; o += A_HEADS * hd
    ka = w[:, o:o + hd]; o += hd
    va = w[:, o:o + hd]; o += hd
    iq = w[:, o:o + IDX_HEADS * hd]; o += IDX_HEADS * hd
    ik = w[:, o:o + hd]; o += hd
    iw = w[:, o:o + IDX_HEADS]; o += IDX_HEADS
    nb = 3 * B_WIDTH + DECAY_LORA + AAA_LORA + GATE_LORA
    wb = w[:, o:o + nb]; o += nb
    qc = w[:, o:o + C_HEADS * hd]; o += C_HEADS * hd
    kc, vc, ksl, vsl, kwn, vwn = (w[:, o + i * hd:o + (i + 1) * hd] for i in range(6)); o += 6 * hd
    gc = w[:, o:o + 3 * C_HEADS]
    heads = lambda m, n: [m[:, i * hd:(i + 1) * hd] for i in range(n)]
    wa = jnp.concatenate([_rope_slot(h) for h in heads(qa, A_HEADS)]
                         + [_rope_slot(h) for h in heads(iq, IDX_HEADS)]
                         + [_rope_slot(ka), _rope_slot(ik), _pad_slot(va), _pad_slot(iw)], axis=1)
    wc = jnp.concatenate([_rope_slot(h) for h in heads(qc, C_HEADS)]
                         + [_rope_slot(ksl), _rope_slot(kwn), jnp.concatenate([kc, vc], axis=1),
                            _pad_slot(vsl), _pad_slot(vwn), _pad_slot(gc)], axis=1)
    return wa.astype(BF16), wb.astype(BF16), wc.astype(BF16)


def _cmp_weights(pe, w1, w2):
    half = CMP_LEN // 2
    hd = HEAD_DIM
    zero = jnp.zeros((half, hd, CMP_HID), F32)

    def expand(lo):
        wk = w1[0].reshape(CMP_LEN, hd, CMP_HID)[lo:lo + half]
        wv = w1[1].reshape(CMP_LEN, hd, CMP_HID)[lo:lo + half]
        k_rows = jnp.concatenate([wk, zero], axis=1)
        v_rows = jnp.concatenate([zero, wv], axis=1)
        return jnp.concatenate([k_rows, v_rows], axis=2).reshape(half * SLOT, 2 * CMP_HID).astype(BF16)

    def pe_row(lo):
        return jnp.concatenate([pe[0, lo:lo + half], pe[1, lo:lo + half]], axis=1).reshape(1, half * SLOT)

    w2k = _rope_slot(w2[0]).astype(BF16)
    w2v = _pad_slot(w2[1]).astype(BF16)
    return pe_row(0), pe_row(half), expand(0), expand(half), w2k, w2v


def _nsa_tables(T):
    n_cmp = (T - CMP_LEN) // CMP_STRIDE + 1
    n_blk = T // SEL_LEN
    starts = np.arange(n_cmp) * CMP_STRIDE
    end_pos = starts + CMP_LEN - 1
    sel_start = np.arange(n_blk) * SEL_LEN
    ovl = np.zeros((SLOT, SLOT), np.float32)
    ovl[:n_cmp, :n_blk] = ((starts[:, None] <= sel_start[None, :] + SEL_LEN - 1)
                           & (end_pos[:, None] >= sel_start[None, :]))
    expand = np.zeros((SLOT, T), np.float32)
    expand[np.arange(T) // SEL_LEN, np.arange(T)] = 1.0
    m = T // CMP_STRIDE
    cse = _rope_table(np.arange(m) * CMP_STRIDE + CMP_LEN - 1)
    return n_cmp, jnp.asarray(ovl, BF16), jnp.asarray(expand, BF16), cse


def _out_weights(w):
    hd = HEAD_DIM
    D = w.shape[1]

    def slots(base, n):
        blocks = [jnp.concatenate([w[base + i * hd:base + (i + 1) * hd], jnp.zeros((SLOT - hd, D), F32)])
                  for i in range(n)]
        return jnp.concatenate(blocks).astype(BF16)

    a0 = 0
    b0 = A_HEADS * hd
    c0 = b0 + B_WIDTH
    return slots(a0, A_HEADS), w[b0:c0].reshape(B_HEADS, hd, D).astype(BF16), slots(c0, C_HEADS)


def kernel(x, c, ada_w, ada_b, norm1_g, w_in, dsa_q_g, dsa_k_g, rwkv_mu, rwkv_w0, rwkv_w2, rwkv_a0, rwkv_a2, rwkv_g2, rwkv_k_k, rwkv_k_a, rwkv_r_k, rwkv_ln_w, rwkv_ln_b, nsa_q_g, nsa_k_g, nsa_pe, nsa_w1, nsa_w2, w_out, norm2_g, ffn_wi, ffn_wo):
    B, T, D = x.shape
    L = w_in.shape[0]
    W = B_WIDTH
    assert T % 256 == 0 and T % RW_CHUNK == 0 and D % SLOT == 0
    mod = _modulation(c, ada_w, ada_b)
    cs = _rope_table(np.arange(T))
    n_cmp, ovl, expand, cse = _nsa_tables(T)
    bd = jnp.asarray(np.kron(np.eye(B_HEADS), np.ones((HEAD_DIM, HEAD_DIM))), BF16)
    lora_pad = jnp.zeros((SLOT - DECAY_LORA, W), F32)
    for l in range(L):
        sh1, sc1, g1, sh2, sc2, g2 = (mod[l, :, i * D:(i + 1) * D] for i in range(6))
        wa, wb, wc = _in_weights(w_in[l])
        za, zb, zc = _in_proj(x, norm1_g[l], sc1, sh1, wa, wb, wc)
        qa, iq, ka, ik, va, iw = _dsa_prep(za, cs, _perm_gain(dsa_q_g[l]), _perm_gain(dsa_k_g[l]))
        oa = _dsa(qa, iq, iw, ka, ik, va)
        w2e = jnp.concatenate([rwkv_w2[l], lora_pad]).astype(BF16)
        a2e = jnp.concatenate([lora_pad, rwkv_a2[l]]).astype(BF16)
        row = lambda p: p.reshape(1, W)
        r, k2, v, kk, bb, lw, gg, bon = _rw_pre(
            zb, rwkv_mu[l].reshape(1, -1), row(rwkv_w0[l]), w2e, row(rwkv_a0[l]), a2e,
            rwkv_g2[l].astype(BF16), row(rwkv_k_k[l]), row(rwkv_k_a[l]), row(rwkv_r_k[l]), bd)
        ry, yc, pm, qm = _rw_chunks(r, k2, v, kk, bb, lw)
        ob = _rw_scan(ry, yc, pm, qm, gg, bon, rwkv_ln_w[l].reshape(B_HEADS, 1, HEAD_DIM),
                      rwkv_ln_b[l].reshape(B_HEADS, 1, HEAD_DIM))
        gkc = _perm_gain(nsa_k_g[l])
        qc, ks, kw, vs, vw, gc = _nsa_prep(zc, cs, _perm_gain(nsa_q_g[l]), gkc)
        kv_slot = (C_HEADS + 2) * SLOT
        rm = zc[:, :, kv_slot:kv_slot + SLOT].reshape(B, T // CMP_STRIDE, CMP_STRIDE * SLOT)
        kc, vc = _nsa_compress(rm, *_cmp_weights(nsa_pe[l], nsa_w1[l], nsa_w2[l]), cse, gkc)
        oc = _nsa(qc, gc, kc, vc, ks, vs, kw, vw, ovl, expand, n_cmp)
        woa, wob, woc = _out_weights(w_out[l])
        x = _out_proj(x, oa, ob, oc, woa, wob, woc, g1)
        x = _ffn(x, norm2_g[l], sc2, sh2, g2, ffn_wi[l].astype(BF16), ffn_wo[l].astype(BF16))
    return x
```
